```python
import functools
import jax, jax.numpy as jnp
from jax import lax
import numpy as np

D_MODEL = 1024
BATCH = 4
SEQ = 4096
DEPTH = 1
DEC_BATCH = 32
DEC_SEQ = 1
PAST_LEN = 16384
PAGE_SIZE = 128

N_HEADS = 8
HEAD_DIM = 64
ATTN_WIDTH = N_HEADS * HEAD_DIM
N_GROUPS = 8
GROUP_DIM = 64
GMLP_WIDTH = N_GROUPS * GROUP_DIM
CHUNK = 128
Q_BLOCK = 128
D_FF = 2816
PLE_DIM = 256
FORGET_BIAS_MIN = 4.0
FORGET_BIAS_MAX = 10.0
EPS = 1e-6
NEG_INF = -1e30
SPLIT_SIZES = (ATTN_WIDTH, ATTN_WIDTH, ATTN_WIDTH, N_HEADS, GMLP_WIDTH, GMLP_WIDTH, D_MODEL, D_MODEL)
D_IN = sum(SPLIT_SIZES)
SPLIT_POINTS = tuple(int(c) for c in np.cumsum(SPLIT_SIZES)[:-1])

kernel_name = 'fox_gmlp_macaron_hybrid_step'


def rms_norm(x, g):
    xf = x.astype(jnp.float32)
    y = xf * lax.rsqrt(jnp.mean(xf * xf, axis=-1, keepdims=True) + EPS)
    return (y * g.astype(jnp.float32)).astype(x.dtype)


def swiglu_ffn(x, g, w_gu, w_down):
    a, b = jnp.split(rms_norm(x, g) @ w_gu, 2, axis=-1)
    return (jax.nn.silu(a) * b) @ w_down


def mixer_inputs(h, w_in, b_forget, q_norm, k_norm, gmlp_v_norm):
    sh = h.shape[:-1]
    q, k, v, f, u, gv, ga, gb = jnp.split(h @ w_in, SPLIT_POINTS, axis=-1)
    q = rms_norm(q.reshape(sh + (N_HEADS, HEAD_DIM)), q_norm)
    k = rms_norm(k.reshape(sh + (N_HEADS, HEAD_DIM)), k_norm)
    v = v.reshape(sh + (N_HEADS, HEAD_DIM))
    logf = jax.nn.log_sigmoid((f + b_forget).astype(jnp.float32))
    u = jax.nn.gelu(u)
    gv = rms_norm(jax.nn.gelu(gv).reshape(sh + (N_GROUPS, GROUP_DIM)), gmlp_v_norm)
    return q, k, v, logf, u, gv, jax.nn.sigmoid(ga), jax.nn.sigmoid(gb)


def fox_attention_prompt(q, k, v, logf):
    b, s = q.shape[:2]
    scale = HEAD_DIM ** -0.5
    f_t = jnp.cumsum(logf, axis=1).transpose(0, 2, 1)
    kpos = jnp.arange(s)

    def block(i):
        start = i * Q_BLOCK
        qb = lax.dynamic_slice_in_dim(q, start, Q_BLOCK, axis=1)
        fb = lax.dynamic_slice_in_dim(f_t, start, Q_BLOCK, axis=2)
        sc = jnp.einsum('bthd,bshd->bhts', qb, k, preferred_element_type=jnp.float32) * scale
        sc = sc + fb[:, :, :, None] - f_t[:, :, None, :]
        qpos = start + jnp.arange(Q_BLOCK)
        sc = jnp.where(kpos[None, :] <= qpos[:, None], sc, NEG_INF)
        p = jax.nn.softmax(sc, axis=-1)
        return jnp.einsum('bhts,bshd->bthd', p.astype(v.dtype), v)

    out = lax.map(block, jnp.arange(s // Q_BLOCK))
    return out.transpose(1, 0, 2, 3, 4).reshape(b, s, ATTN_WIDTH)


def fox_attention_sample(q, k, v, logf, cache_k, cache_v, cache_logf, page_table):
    db, t = q.shape[:2]
    scale = HEAD_DIM ** -0.5
    k_past = cache_k[page_table]
    v_past = cache_v[page_table]
    lf_past = cache_logf[page_table].astype(jnp.float32)
    n_pages, page = lf_past.shape[1], lf_past.shape[2]
    n_past = n_pages * page
    lf_past = lf_past.reshape(db, n_past, N_HEADS)
    suffix = lax.cumsum(lf_past, axis=1, reverse=True) - lf_past
    f_new_t = jnp.cumsum(logf, axis=1).transpose(0, 2, 1)
    s_past = jnp.einsum('bthd,bnphd->bhtnp', q, k_past, preferred_element_type=jnp.float32)
    s_past = s_past.reshape(db, N_HEADS, t, n_past) * scale
    s_past = s_past + suffix.transpose(0, 2, 1)[:, :, None, :] + f_new_t[:, :, :, None]
    s_new = jnp.einsum('bthd,bshd->bhts', q, k, preferred_element_type=jnp.float32) * scale
    s_new = s_new + f_new_t[:, :, :, None] - f_new_t[:, :, None, :]
    s_new = jnp.where(jnp.tril(jnp.ones((t, t), dtype=bool)), s_new, NEG_INF)
    probs = jax.nn.softmax(jnp.concatenate([s_past, s_new], axis=-1), axis=-1)
    p_past = probs[..., :n_past].reshape(db, N_HEADS, t, n_pages, page).astype(v.dtype)
    p_new = probs[..., n_past:].astype(v.dtype)
    out = (jnp.einsum('bhtnp,bnphd->bthd', p_past, v_past)
           + jnp.einsum('bhts,bshd->bthd', p_new, v))
    return out.reshape(db, t, ATTN_WIDTH)


def gmlp_prompt(u, gv, w_s, b_s):
    b, s = gv.shape[:2]
    ws = jnp.where(jnp.tril(jnp.ones((CHUNK, CHUNK), dtype=bool)), w_s, 0.0)
    vc = gv.reshape(b, s // CHUNK, CHUNK, N_GROUPS, GROUP_DIM)
    sp = jnp.einsum('gts,bcsgd->bctgd', ws, vc) + b_s.T[None, None, :, :, None]
    return u * sp.reshape(b, s, GMLP_WIDTH)


def gmlp_sample(u, gv, w_s, b_s):
    db, t = gv.shape[:2]
    ws = jnp.where(jnp.tril(jnp.ones((t, t), dtype=bool)), w_s[:, :t, :t], 0.0)
    sp = jnp.einsum('gts,bsgd->btgd', ws, gv) + b_s[:, :t].T[None, :, :, None]
    return u * sp.reshape(db, t, GMLP_WIDTH)


def per_layer_embedding(x, p_emb, g, w_gate, w_proj):
    return x + jax.nn.sigmoid(rms_norm(x, g) @ w_gate) * (p_emb @ w_proj)


def decoder_layer(x, p_emb, attend, spatial_mix, ffn1_norm, ffn1_w_gu, ffn1_w_down, mix_norm, w_in,
                  b_forget, q_norm, k_norm, gmlp_v_norm, w_proj_attn, w_proj_gmlp, w_out,
                  ffn2_norm, ffn2_w_gu, ffn2_w_down, ple_norm, ple_w_gate, ple_w_proj):
    x = x + 0.5 * swiglu_ffn(x, ffn1_norm, ffn1_w_gu, ffn1_w_down)
    h = rms_norm(x, mix_norm)
    q, k, v, logf, u, gv, gate_a, gate_b = mixer_inputs(h, w_in, b_forget, q_norm, k_norm, gmlp_v_norm)
    a = attend(q, k, v, logf)
    m = spatial_mix(u, gv)
    merged = gate_a * (a @ w_proj_attn) + gate_b * (m @ w_proj_gmlp)
    x = x + merged @ w_out
    x = x + 0.5 * swiglu_ffn(x, ffn2_norm, ffn2_w_gu, ffn2_w_down)
    x = per_layer_embedding(x, p_emb, ple_norm, ple_w_gate, ple_w_proj)
    return x, k, v, logf, gv


def setup_inputs(seed: int = 0) -> dict:
    key = jax.random.key(seed)
    ks = iter(jax.random.split(key, 40))

    def nrm(shape, scale):
        return scale * jax.random.normal(next(ks), shape, jnp.float32)

    def gain(shape):
        return 1.0 + nrm(shape, 0.1)

    n_pages = PAST_LEN // PAGE_SIZE
    n_used = DEC_BATCH * n_pages
    n_pool = n_used + (n_used + 3) // 4
    perm = jax.random.permutation(next(ks), n_pool)
    page_table = perm[:n_used].reshape(DEC_BATCH, n_pages).astype(jnp.int32)
    dscale = D_MODEL ** -0.5
    b_forget = (jnp.linspace(FORGET_BIAS_MIN, FORGET_BIAS_MAX, N_HEADS, dtype=jnp.float32)[None, :]
                + nrm((DEPTH, N_HEADS), 0.1))
    cache_logf = jax.nn.log_sigmoid(b_forget[:, None, None, :] + nrm((DEPTH, n_pool, PAGE_SIZE, N_HEADS), 1.0))
    return {
        'x_prompt': nrm((BATCH, SEQ, D_MODEL), 1.0),
        'x_sample': nrm((DEC_BATCH, DEC_SEQ, D_MODEL), 1.0),
        'cache_k': nrm((DEPTH, n_pool, PAGE_SIZE, N_HEADS, HEAD_DIM), 1.0),
        'cache_v': nrm((DEPTH, n_pool, PAGE_SIZE, N_HEADS, HEAD_DIM), 1.0),
        'cache_logf': cache_logf,
        'page_table': page_table,
        'p_prompt': nrm((DEPTH, BATCH, SEQ, PLE_DIM), 1.0),
        'p_sample': nrm((DEPTH, DEC_BATCH, DEC_SEQ, PLE_DIM), 1.0),
        'ffn1_norm': gain((DEPTH, D_MODEL)),
        'ffn1_w_gu': nrm((DEPTH, D_MODEL, 2 * D_FF), dscale),
        'ffn1_w_down': nrm((DEPTH, D_FF, D_MODEL), D_FF ** -0.5),
        'mix_norm': gain((DEPTH, D_MODEL)),
        'w_in': nrm((DEPTH, D_MODEL, D_IN), dscale),
        'b_forget': b_forget,
        'q_norm': gain((DEPTH, HEAD_DIM)),
        'k_norm': gain((DEPTH, HEAD_DIM)),
        'gmlp_v_norm': gain((DEPTH, GROUP_DIM)),
        'w_spatial': nrm((DEPTH, N_GROUPS, CHUNK, CHUNK), CHUNK ** -0.5),
        'b_spatial': gain((DEPTH, N_GROUPS, CHUNK)),
        'w_proj_attn': nrm((DEPTH, ATTN_WIDTH, D_MODEL), ATTN_WIDTH ** -0.5),
        'w_proj_gmlp': nrm((DEPTH, GMLP_WIDTH, D_MODEL), GMLP_WIDTH ** -0.5),
        'w_out': nrm((DEPTH, D_MODEL, D_MODEL), dscale),
        'ffn2_norm': gain((DEPTH, D_MODEL)),
        'ffn2_w_gu': nrm((DEPTH, D_MODEL, 2 * D_FF), dscale),
        'ffn2_w_down': nrm((DEPTH, D_FF, D_MODEL), D_FF ** -0.5),
        'ple_norm': gain((DEPTH, D_MODEL)),
        'ple_w_gate': nrm((DEPTH, D_MODEL, D_MODEL), dscale),
        'ple_w_proj': nrm((DEPTH, PLE_DIM, D_MODEL), PLE_DIM ** -0.5),
    }


def reference(x_prompt, x_sample, cache_k, cache_v, cache_logf, page_table, p_prompt, p_sample,
              ffn1_norm, ffn1_w_gu, ffn1_w_down, mix_norm, w_in, b_forget, q_norm, k_norm,
              gmlp_v_norm, w_spatial, b_spatial, w_proj_attn, w_proj_gmlp, w_out,
              ffn2_norm, ffn2_w_gu, ffn2_w_down, ple_norm, ple_w_gate, ple_w_proj):
    xp, xs = x_prompt, x_sample
    kp_l, vp_l, fp_l, ks_l, vs_l, fs_l, gs_l = [], [], [], [], [], [], []
    for i in range(DEPTH):
        lw = (ffn1_norm[i], ffn1_w_gu[i], ffn1_w_down[i], mix_norm[i], w_in[i], b_forget[i],
              q_norm[i], k_norm[i], gmlp_v_norm[i], w_proj_attn[i], w_proj_gmlp[i], w_out[i],
              ffn2_norm[i], ffn2_w_gu[i], ffn2_w_down[i], ple_norm[i], ple_w_gate[i], ple_w_proj[i])
        mix_p = functools.partial(gmlp_prompt, w_s=w_spatial[i], b_s=b_spatial[i])
        mix_s = functools.partial(gmlp_sample, w_s=w_spatial[i], b_s=b_spatial[i])
        att_s = functools.partial(fox_attention_sample, cache_k=cache_k[i], cache_v=cache_v[i],
                                  cache_logf=cache_logf[i], page_table=page_table)
        xp, kp, vp, fp, _ = decoder_layer(xp, p_prompt[i], fox_attention_prompt, mix_p, *lw)
        xs, k_s, v_s, f_s, g_s = decoder_layer(xs, p_sample[i], att_s, mix_s, *lw)
        kp_l.append(kp); vp_l.append(vp); fp_l.append(fp)
        ks_l.append(k_s); vs_l.append(v_s); fs_l.append(f_s); gs_l.append(g_s)
    k_prompt = jnp.stack(kp_l)
    v_prompt = jnp.stack(vp_l)
    logf_prompt = jnp.stack(fp_l)
    k_sample = jnp.stack(ks_l)
    v_sample = jnp.stack(vs_l)
    logf_sample = jnp.stack(fs_l)
    gmlp_v_sample = jnp.stack(gs_l)
    return (xp, xs, k_prompt, v_prompt, logf_prompt, k_sample, v_sample, logf_sample, gmlp_v_sample)
```

```python
import functools

import jax
import jax.numpy as jnp
import numpy as np
from jax import lax
from jax.experimental import pallas as pl
from jax.experimental.pallas import tpu as pltpu

EPS = 1e-6
NEG_INF = -1e30
LOG2E = 1.4426950408889634
LANES = 128
F32 = jnp.float32
BF16 = jnp.bfloat16
VMEM_LIMIT = 56 * 1024 * 1024


def _dot(a, b):
    return jnp.dot(a, b, preferred_element_type=F32)


def _dot_nt(a, b):
    return lax.dot_general(a, b, (((1,), (1,)), ((), ())), preferred_element_type=F32)


def _rms(x, g):
    ms = jnp.mean(x * x, axis=-1, keepdims=True)
    return x * lax.rsqrt(ms + EPS) * g


def _group_rms(x, g, bd):
    ms = _dot((x * x).astype(BF16), bd)
    return x * lax.rsqrt(ms + EPS) * g


def _split3(x):
    hi = x.astype(BF16)
    r = x - hi.astype(F32)
    mid = r.astype(BF16)
    lo = (r - mid.astype(F32)).astype(BF16)
    return hi, mid, lo


def _const_spec(shape):
    nd = len(shape)
    return pl.BlockSpec(shape, lambda *_: (0,) * nd, pipeline_mode=pl.Buffered(1))


def _params(n_axes):
    return pltpu.CompilerParams(dimension_semantics=("arbitrary",) * n_axes,
                                vmem_limit_bytes=VMEM_LIMIT)


def _ffn_kernel(x_ref, g_ref, wgu_ref, wd_ref, o_ref, act_ref, *, d_ff, chunk):
    x = x_ref[...]
    h = _rms(x, g_ref[...]).astype(BF16)
    for c in range(d_ff // chunk):
        a = _dot(h, wgu_ref[:, c * chunk:(c + 1) * chunk])
        b = _dot(h, wgu_ref[:, d_ff + c * chunk:d_ff + (c + 1) * chunk])
        act_ref[:, c * chunk:(c + 1) * chunk] = (jax.nn.silu(a) * b).astype(BF16)
    o_ref[...] = x + 0.5 * _dot(act_ref[...], wd_ref[...])


def _ffn_call(x, g, wgu, wd, tm):
    n, d = x.shape
    d_ff = wd.shape[0]
    chunk = 256 if d_ff % 256 == 0 else LANES
    return pl.pallas_call(
        functools.partial(_ffn_kernel, d_ff=d_ff, chunk=chunk),
        out_shape=jax.ShapeDtypeStruct((n, d), F32),
        grid=(n // tm,),
        in_specs=[pl.BlockSpec((tm, d), lambda i: (i, 0)),
                  _const_spec(g.shape), _const_spec(wgu.shape), _const_spec(wd.shape)],
        out_specs=pl.BlockSpec((tm, d), lambda i: (i, 0)),
        scratch_shapes=[pltpu.VMEM((tm, d_ff), BF16)],
        compiler_params=_params(1),
        name="ffn",
    )(x, g, wgu, wd)


def _mixer_common(x_ref, g_ref, wqkv_ref, wf_ref, bf_ref, wugv_ref, qg_ref, kg_ref, gvg_ref, bd_ref):
    h = _rms(x_ref[...], g_ref[...]).astype(BF16)
    aw = qg_ref.shape[1]
    qkv = _dot(h, wqkv_ref[...])
    q = _group_rms(qkv[:, :aw], qg_ref[...], bd_ref[...])
    k = _group_rms(qkv[:, aw:2 * aw], kg_ref[...], bd_ref[...])
    v = qkv[:, 2 * aw:]
    logf = jax.nn.log_sigmoid(_dot(h, wf_ref[...]) + bf_ref[...])
    ugv = _dot(h, wugv_ref[...])
    gw = gvg_ref.shape[1]
    u = jax.nn.gelu(ugv[:, :gw])
    gv = _group_rms(jax.nn.gelu(ugv[:, gw:]), gvg_ref[...], bd_ref[...])
    return q, k, v, logf, u, gv


def _mixer_prompt_kernel(x_ref, g_ref, wqkv_ref, wf_ref, bf_ref, wugv_ref, qg_ref, kg_ref, gvg_ref,
                         bd_ref, tri_ref, selq_ref, selk_ref, cq_ref, ck_ref, ws_ref, bs_ref,
                         qp_ref, qa_ref, k_ref, kp_ref, ka_ref, v_ref, vp_ref, lf_ref, mg_ref,
                         carry_ref, *, tiles_per_seq, n_heads, chunk):
    q, k, v, logf, u, gv = _mixer_common(x_ref, g_ref, wqkv_ref, wf_ref, bf_ref, wugv_ref,
                                         qg_ref, kg_ref, gvg_ref, bd_ref)
    tm = q.shape[0]
    qp_ref[...] = q.astype(BF16)
    k_ref[...] = k
    kp_ref[...] = k.astype(BF16)
    v_ref[...] = v
    vp_ref[...] = v.astype(BF16)
    lf_ref[...] = logf[:, :n_heads]

    @pl.when(pl.program_id(0) % tiles_per_seq == 0)
    def _():
        carry_ref[...] = jnp.zeros_like(carry_ref)

    lane = lax.broadcasted_iota(jnp.int32, logf.shape, 1)
    lf = jnp.where(lane < n_heads, logf, 0.0)
    pieces = jnp.concatenate(_split3(lf), axis=-1)
    cs = _dot(tri_ref[...], pieces)
    f_cum = cs[:, :LANES] + cs[:, LANES:2 * LANES] + cs[:, 2 * LANES:] + carry_ref[...]
    carry_ref[...] = f_cum[tm - 1:tm, :]
    fp = jnp.concatenate(_split3(f_cum * LOG2E), axis=-1)
    qa_ref[...] = (_dot(fp, selq_ref[...]) + cq_ref[...]).astype(BF16)
    ka_ref[...] = (_dot(fp, selk_ref[...]) + ck_ref[...]).astype(BF16)

    gvb = gv.astype(BF16)
    r_i = lax.broadcasted_iota(jnp.int32, (chunk, chunk), 0)
    c_i = lax.broadcasted_iota(jnp.int32, (chunk, chunk), 1)
    lane_c = lax.broadcasted_iota(jnp.int32, (chunk, LANES), 1)
    n_groups = ws_ref.shape[0]
    gdim = gvb.shape[1] // n_groups
    per_blk = LANES // gdim
    ws = [jnp.where(c_i <= r_i, ws_ref[g], jnp.zeros((), BF16)) for g in range(n_groups)]
    for c in range(tm // chunk):
        rows = slice(c * chunk, (c + 1) * chunk)
        for jb in range(gvb.shape[1] // LANES):
            cols = slice(jb * LANES, (jb + 1) * LANES)
            blk = gvb[rows, cols]
            sp = _dot(ws[jb * per_blk], blk)
            for gi in range(1, per_blk):
                sp = jnp.where(lane_c >= gi * gdim, _dot(ws[jb * per_blk + gi], blk), sp)
            mg_ref[rows, cols] = (u[rows, cols] * (sp + bs_ref[:, cols])).astype(BF16)


def _mixer_sample_kernel(x_ref, g_ref, wqkv_ref, wf_ref, bf_ref, wugv_ref, qg_ref, kg_ref, gvg_ref,
                         bd_ref, w00_ref, b0_ref,
                         q_ref, k_ref, v_ref, lf_ref, gv_ref, mg_ref):
    q, k, v, logf, u, gv = _mixer_common(x_ref, g_ref, wqkv_ref, wf_ref, bf_ref, wugv_ref,
                                         qg_ref, kg_ref, gvg_ref, bd_ref)
    q_ref[...] = q
    k_ref[...] = k
    v_ref[...] = v
    lf_ref[...] = logf
    gv_ref[...] = gv
    mg_ref[...] = (u * (gv * w00_ref[...] + b0_ref[...])).astype(BF16)


def _attn_kernel(qp_ref, qa_ref, kp_ref, ka_ref, vp_ref, o_ref, m_ref, l_ref, acc_ref, *, tq, hd):
    i = pl.program_id(2)
    n_sub = LANES // hd
    q = jnp.concatenate([qp_ref[0], qa_ref[0]], axis=-1)
    lane_q = lax.broadcasted_iota(jnp.int32, q.shape, 1) & (LANES - 1)
    qs = [jnp.where((lane_q >= h * hd) & (lane_q < (h + 1) * hd), q, jnp.zeros((), BF16))
          for h in range(n_sub)]
    m_ref[...] = jnp.full_like(m_ref, NEG_INF)
    l_ref[...] = jnp.zeros_like(l_ref)
    acc_ref[...] = jnp.zeros_like(acc_ref)

    def step(j, masked):
        start = pl.multiple_of(j * tq, tq)
        kc = jnp.concatenate([kp_ref[0, pl.ds(start, tq), :], ka_ref[0, pl.ds(start, tq), :]], axis=-1)
        vc = vp_ref[0, pl.ds(start, tq), :]
        for h in range(n_sub):
            s = _dot_nt(qs[h], kc)
            if masked:
                r_i = lax.broadcasted_iota(jnp.int32, s.shape, 0)
                c_i = lax.broadcasted_iota(jnp.int32, s.shape, 1)
                s = jnp.where(c_i <= r_i, s, NEG_INF)
            m_prev = m_ref[h]
            m_next = jnp.maximum(m_prev, jnp.max(s, axis=-1, keepdims=True))
            alpha = jnp.exp2(m_prev - m_next)
            p = jnp.exp2(s - jnp.tile(m_next, (1, tq // LANES)))
            l_ref[h] = alpha * l_ref[h] + jnp.sum(p, axis=-1, keepdims=True)
            acc_ref[h] = alpha * acc_ref[h] + _dot(p.astype(BF16), vc)
            m_ref[h] = m_next

    def body(j, c):
        step(j, False)
        return c

    lax.fori_loop(0, i, body, 0)
    step(i, True)

    lane_o = lax.broadcasted_iota(jnp.int32, (tq, LANES), 1)
    out = acc_ref[0] / l_ref[0]
    for h in range(1, n_sub):
        out = jnp.where(lane_o >= h * hd, acc_ref[h] / l_ref[h], out)
    o_ref[0] = out.astype(o_ref.dtype)


def _attn_call(qp, qa, kp, ka, vp, tq, hd):
    b, s, w = qp.shape
    n_sub = LANES // hd
    q_spec = pl.BlockSpec((1, tq, LANES), lambda bi, hp, i: (bi, i, hp))
    kv_spec = pl.BlockSpec((1, s, LANES), lambda bi, hp, i: (bi, 0, hp))
    return pl.pallas_call(
        functools.partial(_attn_kernel, tq=tq, hd=hd),
        out_shape=jax.ShapeDtypeStruct((b, s, w), BF16),
        grid=(b, w // LANES, s // tq),
        in_specs=[q_spec, q_spec, kv_spec, kv_spec, kv_spec],
        out_specs=q_spec,
        scratch_shapes=[pltpu.VMEM((n_sub, tq, LANES), F32), pltpu.VMEM((n_sub, tq, LANES), F32),
                        pltpu.VMEM((n_sub, tq, LANES), F32)],
        compiler_params=_params(3),
        name="fox_attn_prompt",
    )(qp, qa, kp, ka, vp)


def _decode_kernel(pt_ref, q_ref, kn_ref, vn_ref, fn_ref, ltri_ref, *refs, g_pages, page, n_heads, hd):
    k_refs = refs[:g_pages]
    v_refs = refs[g_pages:2 * g_pages]
    lf_refs = refs[2 * g_pages:3 * g_pages]
    o_ref, m_ref, l_ref, acc_ref, car_ref = refs[3 * g_pages:]
    g = pl.program_id(1)
    width = n_heads * hd
    row = lax.broadcasted_iota(jnp.int32, (n_heads, width), 0)
    col = lax.broadcasted_iota(jnp.int32, (n_heads, width), 1)
    head_mask = (col >= row * hd) & (col < (row + 1) * hd)
    q = q_ref[0]
    qbd = jnp.where(head_mask, jnp.broadcast_to(q, (n_heads, width)), 0.0)

    @pl.when(g == 0)
    def _():
        m_ref[...] = jnp.full_like(m_ref, NEG_INF)
        l_ref[...] = jnp.zeros_like(l_ref)
        acc_ref[...] = jnp.zeros_like(acc_ref)
        car_ref[...] = jnp.broadcast_to(fn_ref[0], car_ref.shape)

    qb = qbd.astype(BF16)
    lf = jnp.concatenate([r[0] for r in lf_refs], axis=0)
    pieces = jnp.concatenate(_split3(lf), axis=0)
    both = _dot(pieces, ltri_ref[...])
    gh = g_pages * n_heads
    both = both[:gh] + both[gh:2 * gh] + both[2 * gh:]
    inner = both[:, :page]
    total = both[:, page:]
    carry = car_ref[...]
    bias = [None] * g_pages
    for r in reversed(range(g_pages)):
        bias[r] = inner[r * n_heads:(r + 1) * n_heads] + carry
        carry = carry + total[r * n_heads:(r + 1) * n_heads]
    car_ref[...] = carry

    s = [_dot_nt(qb, k_refs[r][0].astype(BF16)) + bias[r] * LOG2E for r in range(g_pages)]
    s = jnp.concatenate(s, axis=-1)
    m_prev = m_ref[...]
    m_next = jnp.maximum(m_prev, jnp.max(s, axis=-1, keepdims=True))
    alpha = jnp.exp2(m_prev - m_next)
    p = jnp.exp2(s - jnp.tile(m_next, (1, g_pages * page // LANES)))
    l_ref[...] = alpha * l_ref[...] + jnp.sum(p, axis=-1, keepdims=True)
    pv = _dot(p[:, :page].astype(BF16), v_refs[0][0].astype(BF16))
    for r in range(1, g_pages):
        pv = pv + _dot(p[:, r * page:(r + 1) * page].astype(BF16), v_refs[r][0].astype(BF16))
    acc_ref[...] = jnp.tile(alpha, (1, width // LANES)) * acc_ref[...] + pv
    m_ref[...] = m_next

    @pl.when(g == pl.num_programs(1) - 1)
    def _():
        s_new = jnp.sum(qbd * kn_ref[0], axis=-1, keepdims=True)
        m_old = m_ref[...]
        m_fin = jnp.maximum(m_old, s_new)
        a_fin = jnp.exp2(m_old - m_fin)
        p_new = jnp.exp2(s_new - m_fin)
        l_fin = a_fin * l_ref[...] + p_new
        acc = (jnp.tile(a_fin, (1, width // LANES)) * acc_ref[...]
               + jnp.tile(p_new, (1, width // LANES)) * vn_ref[0])
        out = jnp.where(head_mask, acc / jnp.tile(l_fin, (1, width // LANES)), 0.0)
        o_ref[0] = jnp.sum(out, axis=0, keepdims=True).astype(o_ref.dtype)


def _decode_call(page_table, q, kn, vn, fn, cache_k, cache_v, cache_lf_t, g_pages, n_heads, hd):
    db, n_pages = page_table.shape
    n_pool, page, width = cache_k.shape
    n_groups = n_pages // g_pages
    ltri = np.concatenate([np.tril(np.ones((page, page), np.float32), -1),
                           np.ones((page, page), np.float32)], axis=1)
    ltri = jnp.asarray(ltri, BF16)

    def page_map(r):
        return lambda b, g, pt: (pt[b, (n_groups - 1 - g) * g_pages + r], 0, 0)

    row_spec = pl.BlockSpec((1, 1, width), lambda b, g, pt: (b, 0, 0))
    in_specs = ([row_spec, row_spec, row_spec,
                 pl.BlockSpec((1, n_heads, LANES), lambda b, g, pt: (b, 0, 0)),
                 pl.BlockSpec(ltri.shape, lambda b, g, pt: (0, 0))]
                + [pl.BlockSpec((1, page, width), page_map(r)) for r in range(g_pages)]
                + [pl.BlockSpec((1, page, width), page_map(r)) for r in range(g_pages)]
                + [pl.BlockSpec((1, n_heads, page), page_map(r)) for r in range(g_pages)])
    return pl.pallas_call(
        functools.partial(_decode_kernel, g_pages=g_pages, page=page, n_heads=n_heads, hd=hd),
        out_shape=jax.ShapeDtypeStruct((db, 1, width), BF16),
        grid_spec=pltpu.PrefetchScalarGridSpec(
            num_scalar_prefetch=1,
            grid=(db, n_groups),
            in_specs=in_specs,
            out_specs=pl.BlockSpec((1, 1, width), lambda b, g, pt: (b, 0, 0)),
            scratch_shapes=[pltpu.VMEM((n_heads, LANES), F32), pltpu.VMEM((n_heads, LANES), F32),
                            pltpu.VMEM((n_heads, width), F32), pltpu.VMEM((n_heads, LANES), F32)]),
        compiler_params=_params(2),
        name="fox_attn_sample",
    )(page_table, q, kn, vn, fn, ltri, *([cache_k] * g_pages), *([cache_v] * g_pages),
      *([cache_lf_t] * g_pages))


def _merge_kernel(x_ref, a_ref, mg_ref, g_ref, wga_ref, wgb_ref, wpa_ref, wpg_ref, wo_ref, o_ref):
    x = x_ref[...]
    h = _rms(x, g_ref[...]).astype(BF16)
    ga = jax.nn.sigmoid(_dot(h, wga_ref[...]))
    gb = jax.nn.sigmoid(_dot(h, wgb_ref[...]))
    merged = ga * _dot(a_ref[...], wpa_ref[...]) + gb * _dot(mg_ref[...], wpg_ref[...])
    o_ref[...] = x + _dot(merged.astype(BF16), wo_ref[...])


def _merge_call(x, a, mg, g, wga, wgb, wpa, wpg, wo, tm):
    n, d = x.shape
    w = a.shape[1]
    row = lambda cols: pl.BlockSpec((tm, cols), lambda i: (i, 0))
    return pl.pallas_call(
        _merge_kernel,
        out_shape=jax.ShapeDtypeStruct((n, d), F32),
        grid=(n // tm,),
        in_specs=[row(d), row(w), row(mg.shape[1])] + [_const_spec(t.shape) for t in (g, wga, wgb, wpa, wpg, wo)],
        out_specs=row(d),
        compiler_params=_params(1),
        name="merge_out",
    )(x, a, mg, g, wga, wgb, wpa, wpg, wo)


def _ple_kernel(x_ref, p_ref, g_ref, wg_ref, wp_ref, o_ref):
    x = x_ref[...]
    gate = jax.nn.sigmoid(_dot(_rms(x, g_ref[...]).astype(BF16), wg_ref[...]))
    o_ref[...] = x + gate * _dot(p_ref[...].astype(BF16), wp_ref[...])


def _ple_call(x, p, g, wg, wp, tm):
    n, d = x.shape
    row = lambda cols: pl.BlockSpec((tm, cols), lambda i: (i, 0))
    return pl.pallas_call(
        _ple_kernel,
        out_shape=jax.ShapeDtypeStruct((n, d), F32),
        grid=(n // tm,),
        in_specs=[row(d), row(p.shape[1])] + [_const_spec(t.shape) for t in (g, wg, wp)],
        out_specs=row(d),
        compiler_params=_params(1),
        name="ple",
    )(x, p, g, wg, wp)


def _block_diag_mean(width, group):
    idx = np.arange(width) // group
    return jnp.asarray((idx[:, None] == idx[None, :]).astype(np.float32) / group, BF16)


def _forget_selectors(n_heads, hd):
    width = n_heads * hd
    selq = np.zeros((3 * LANES, width), np.float32)
    selk = np.zeros((3 * LANES, width), np.float32)
    cq = np.zeros((1, width), np.float32)
    ck = np.zeros((1, width), np.float32)
    for h in range(n_heads):
        for p in range(3):
            selq[p * LANES + h, h * hd + p] = 1.0
            selk[p * LANES + h, h * hd + 3 + p] = -1.0
            cq[0, h * hd + 3 + p] = 1.0
            ck[0, h * hd + p] = 1.0
    return (jnp.asarray(selq, BF16), jnp.asarray(selk, BF16), jnp.asarray(cq), jnp.asarray(ck))


def kernel(x_prompt, x_sample, cache_k, cache_v, cache_logf, page_table, p_prompt, p_sample,
           ffn1_norm, ffn1_w_gu, ffn1_w_down, mix_norm, w_in, b_forget, q_norm, k_norm,
           gmlp_v_norm, w_spatial, b_spatial, w_proj_attn, w_proj_gmlp, w_out,
           ffn2_norm, ffn2_w_gu, ffn2_w_down, ple_norm, ple_w_gate, ple_w_proj):
    depth = ffn1_norm.shape[0]
    bsz, seq, d = x_prompt.shape
    db = x_sample.shape[0]
    assert x_sample.shape[1] == 1, "the sample group decodes one token per step"
    n_heads, hd = cache_k.shape[3], cache_k.shape[4]
    aw = n_heads * hd
    n_groups, chunk = w_spatial.shape[1], w_spatial.shape[2]
    gw = w_proj_gmlp.shape[1]
    gdim = gw // n_groups
    page = cache_k.shape[2]
    n_pool = cache_k.shape[1]
    assert hd == gdim and aw == gw and LANES % hd == 0 and n_heads <= LANES

    tm = 512
    tq = 512
    g_pages = 8
    scale = hd ** -0.5
    tri = jnp.asarray(np.tril(np.ones((tm, tm), np.float32)), BF16)
    bd = _block_diag_mean(aw, hd)
    selq, selk, cq, ck = _forget_selectors(n_heads, hd)

    xp = x_prompt.reshape(bsz * seq, d)
    xs = x_sample.reshape(db, d)
    outs = {k: [] for k in ("kp", "vp", "fp", "ks", "vs", "fs", "gs")}
    for li in range(depth):
        bf = lambda t: t[li].astype(BF16)
        row = lambda t: t[li].reshape(1, -1).astype(F32)
        tile_h = lambda t, mult=1.0: jnp.tile(t[li].astype(F32) * mult, n_heads).reshape(1, -1)
        w1gu, w1d, w2gu, w2d = bf(ffn1_w_gu), bf(ffn1_w_down), bf(ffn2_w_gu), bf(ffn2_w_down)
        wi = w_in[li]
        o = 0
        wqkv = wi[:, o:o + 3 * aw].astype(BF16); o += 3 * aw
        wf = jnp.pad(wi[:, o:o + n_heads], ((0, 0), (0, LANES - n_heads))).astype(BF16); o += n_heads
        wugv = wi[:, o:o + 2 * gw].astype(BF16); o += 2 * gw
        wga = wi[:, o:o + d].astype(BF16); o += d
        wgb = wi[:, o:o + d].astype(BF16)
        bfg = jnp.pad(b_forget[li].astype(F32), (0, LANES - n_heads)).reshape(1, LANES)
        qg, kg, gvg = tile_h(q_norm, scale * LOG2E), tile_h(k_norm), tile_h(gmlp_v_norm)
        ws = bf(w_spatial)
        bs = jnp.repeat(b_spatial[li].astype(F32).T, gdim, axis=1)
        w00 = jnp.repeat(w_spatial[li, :, 0, 0].astype(F32), gdim).reshape(1, gw)
        b0 = jnp.repeat(b_spatial[li, :, 0].astype(F32), gdim).reshape(1, gw)
        wpa, wpg, wo = bf(w_proj_attn), bf(w_proj_gmlp), bf(w_out)
        plg, plp = bf(ple_w_gate), bf(ple_w_proj)
        mixer_w = (row(mix_norm), wqkv, wf, bfg, wugv, qg, kg, gvg, bd)

        n = bsz * seq
        x1 = _ffn_call(xp, row(ffn1_norm), w1gu, w1d, tm)
        rowspec = lambda cols: pl.BlockSpec((tm, cols), lambda i: (i, 0))
        consts = mixer_w + (tri, selq, selk, cq, ck, ws, bs)
        sds = lambda cols, dt: jax.ShapeDtypeStruct((n, cols), dt)
        qp, qa, k32, kp, ka, v32, vp, lfp, mg = pl.pallas_call(
            functools.partial(_mixer_prompt_kernel, tiles_per_seq=seq // tm, n_heads=n_heads, chunk=chunk),
            out_shape=(sds(aw, BF16), sds(aw, BF16), sds(aw, F32), sds(aw, BF16), sds(aw, BF16),
                       sds(aw, F32), sds(aw, BF16), sds(n_heads, F32), sds(gw, BF16)),
            grid=(n // tm,),
            in_specs=[rowspec(d)] + [_const_spec(t.shape) for t in consts],
            out_specs=tuple(rowspec(c) for c in (aw, aw, aw, aw, aw, aw, aw, n_heads, gw)),
            scratch_shapes=[pltpu.VMEM((1, LANES), F32)],
            compiler_params=_params(1),
            name="mixer_prompt",
        )(x1, *consts)
        r3 = lambda t: t.reshape(bsz, seq, aw)
        a = _attn_call(r3(qp), r3(qa), r3(kp), r3(ka), r3(vp), tq, hd).reshape(n, aw)
        x2 = _merge_call(x1, a, mg, row(mix_norm), wga, wgb, wpa, wpg, wo, tm)
        x3 = _ffn_call(x2, row(ffn2_norm), w2gu, w2d, tm)
        xp = _ple_call(x3, p_prompt[li].reshape(n, -1), row(ple_norm), plg, plp, tm)
        outs["kp"].append(k32.reshape(bsz, seq, n_heads, hd))
        outs["vp"].append(v32.reshape(bsz, seq, n_heads, hd))
        outs["fp"].append(lfp.reshape(bsz, seq, n_heads))

        x1s = _ffn_call(xs, row(ffn1_norm), w1gu, w1d, db)
        sconsts = mixer_w + (w00, b0)
        full = lambda cols: pl.BlockSpec((db, cols), lambda i: (0, 0))
        ssd = lambda cols, dt: jax.ShapeDtypeStruct((db, cols), dt)
        qs, ksn, vsn, lfs, gvs, mgs = pl.pallas_call(
            _mixer_sample_kernel,
            out_shape=(ssd(aw, F32), ssd(aw, F32), ssd(aw, F32), ssd(LANES, F32), ssd(gw, F32), ssd(gw, BF16)),
            grid=(1,),
            in_specs=[full(d)] + [_const_spec(t.shape) for t in sconsts],
            out_specs=tuple(full(c) for c in (aw, aw, aw, LANES, gw, gw)),
            compiler_params=_params(1),
            name="mixer_sample",
        )(x1s, *sconsts)
        fn = jnp.broadcast_to(lfs[:, :n_heads, None], (db, n_heads, LANES))
        cache_lf_t = jnp.swapaxes(cache_logf[li].astype(F32), 1, 2)
        a_s = _decode_call(page_table, qs.reshape(db, 1, aw), ksn.reshape(db, 1, aw),
                           vsn.reshape(db, 1, aw), fn,
                           cache_k[li].reshape(n_pool, page, aw), cache_v[li].reshape(n_pool, page, aw),
                           cache_lf_t, g_pages, n_heads, hd).reshape(db, aw)
        x2s = _merge_call(x1s, a_s, mgs, row(mix_norm), wga, wgb, wpa, wpg, wo, db)
        x3s = _ffn_call(x2s, row(ffn2_norm), w2gu, w2d, db)
        xs = _ple_call(x3s, p_sample[li].reshape(db, -1), row(ple_norm), plg, plp, db)
        outs["ks"].append(ksn.reshape(db, 1, n_heads, hd))
        outs["vs"].append(vsn.reshape(db, 1, n_heads, hd))
        outs["fs"].append(lfs[:, :n_heads].reshape(db, 1, n_heads))
        outs["gs"].append(gvs.reshape(db, 1, n_groups, gdim))

    st = lambda key: jnp.stack(outs[key])
    return (xp.reshape(bsz, seq, d), xs.reshape(db, 1, d), st("kp"), st("vp"), st("fp"),
            st("ks"), st("vs"), st("fs"), st("gs"))
```

```python
import functools

import jax
import jax.numpy as jnp
import numpy as np
from jax import lax
from jax.experimental import pallas as pl
from jax.experimental.pallas import tpu as pltpu

EPS = 1e-6
NEG_INF = -1e30
LOG2E = 1.4426950408889634
LANES = 128
F32 = jnp.float32
BF16 = jnp.bfloat16
VMEM_LIMIT = 56 * 1024 * 1024


def _dot(a, b):
    return jnp.dot(a, b, preferred_element_type=F32)


def _dot_nt(a, b):
    return lax.dot_general(a, b, (((1,), (1,)), ((), ())), preferred_element_type=F32)


def _rms(x, g):
    ms = jnp.mean(x * x, axis=-1, keepdims=True)
    return x * lax.rsqrt(ms + EPS) * g


def _group_rms(x, g, bd):
    ms = _dot((x * x).astype(BF16), bd)
    return x * lax.rsqrt(ms + EPS) * g


def _split3(x):
    hi = x.astype(BF16)
    r = x - hi.astype(F32)
    mid = r.astype(BF16)
    lo = (r - mid.astype(F32)).astype(BF16)
    return hi, mid, lo


def _const_spec(shape):
    nd = len(shape)
    return pl.BlockSpec(shape, lambda *_: (0,) * nd, pipeline_mode=pl.Buffered(1))


def _params(n_axes):
    return pltpu.CompilerParams(dimension_semantics=("arbitrary",) * n_axes,
                                vmem_limit_bytes=VMEM_LIMIT)


def _ffn_kernel(x_ref, g_ref, wgu_ref, wd_ref, o_ref, act_ref, *, d_ff, chunk):
    x = x_ref[...]
    h = _rms(x, g_ref[...]).astype(BF16)
    for c in range(d_ff // chunk):
        a = _dot(h, wgu_ref[:, c * chunk:(c + 1) * chunk])
        b = _dot(h, wgu_ref[:, d_ff + c * chunk:d_ff + (c + 1) * chunk])
        act_ref[:, c * chunk:(c + 1) * chunk] = (jax.nn.silu(a) * b).astype(BF16)
    o_ref[...] = x + 0.5 * _dot(act_ref[...], wd_ref[...])


def _ffn_call(x, g, wgu, wd, tm):
    n, d = x.shape
    d_ff = wd.shape[0]
    chunk = 256 if d_ff % 256 == 0 else LANES
    return pl.pallas_call(
        functools.partial(_ffn_kernel, d_ff=d_ff, chunk=chunk),
        out_shape=jax.ShapeDtypeStruct((n, d), F32),
        grid=(n // tm,),
        in_specs=[pl.BlockSpec((tm, d), lambda i: (i, 0)),
                  _const_spec(g.shape), _const_spec(wgu.shape), _const_spec(wd.shape)],
        out_specs=pl.BlockSpec((tm, d), lambda i: (i, 0)),
        scratch_shapes=[pltpu.VMEM((tm, d_ff), BF16)],
        compiler_params=_params(1),
        name="ffn",
    )(x, g, wgu, wd)


def _mixer_common(x_ref, g_ref, wqkv_ref, wf_ref, bf_ref, wugv_ref, qg_ref, kg_ref, gvg_ref, bd_ref):
    h = _rms(x_ref[...], g_ref[...]).astype(BF16)
    aw = qg_ref.shape[1]
    qkv = _dot(h, wqkv_ref[...])
    q = _group_rms(qkv[:, :aw], qg_ref[...], bd_ref[...])
    k = _group_rms(qkv[:, aw:2 * aw], kg_ref[...], bd_ref[...])
    v = qkv[:, 2 * aw:]
    logf = jax.nn.log_sigmoid(_dot(h, wf_ref[...]) + bf_ref[...])
    ugv = _dot(h, wugv_ref[...])
    gw = gvg_ref.shape[1]
    u = jax.nn.gelu(ugv[:, :gw])
    gv = _group_rms(jax.nn.gelu(ugv[:, gw:]), gvg_ref[...], bd_ref[...])
    return q, k, v, logf, u, gv


def _mixer_prompt_kernel(x_ref, g_ref, wqkv_ref, wf_ref, bf_ref, wugv_ref, qg_ref, kg_ref, gvg_ref,
                         bd_ref, tri_ref, selq_ref, selk_ref, cq_ref, ck_ref, ws_ref, bs_ref,
                         qp_ref, qa_ref, k_ref, kp_ref, ka_ref, v_ref, vp_ref, lf_ref, mg_ref,
                         carry_ref, *, tiles_per_seq, n_heads, chunk):
    q, k, v, logf, u, gv = _mixer_common(x_ref, g_ref, wqkv_ref, wf_ref, bf_ref, wugv_ref,
                                         qg_ref, kg_ref, gvg_ref, bd_ref)
    tm = q.shape[0]
    qp_ref[...] = q.astype(BF16)
    k_ref[0] = k.T
    kp_ref[...] = k.astype(BF16)
    v_ref[0] = v.T
    vp_ref[...] = v.astype(BF16)
    lf_ref[0] = logf.T[:n_heads]

    @pl.when(pl.program_id(0) % tiles_per_seq == 0)
    def _():
        carry_ref[...] = jnp.zeros_like(carry_ref)

    lane = lax.broadcasted_iota(jnp.int32, logf.shape, 1)
    lf = jnp.where(lane < n_heads, logf, 0.0)
    pieces = jnp.concatenate(_split3(lf), axis=-1)
    cs = _dot(tri_ref[...], pieces)
    f_cum = cs[:, :LANES] + cs[:, LANES:2 * LANES] + cs[:, 2 * LANES:] + carry_ref[...]
    carry_ref[...] = f_cum[tm - 1:tm, :]
    fp = jnp.concatenate(_split3(f_cum * LOG2E), axis=-1)
    qa_ref[...] = (_dot(fp, selq_ref[...]) + cq_ref[...]).astype(BF16)
    ka_ref[...] = (_dot(fp, selk_ref[...]) + ck_ref[...]).astype(BF16)

    gvb = gv.astype(BF16)
    r_i = lax.broadcasted_iota(jnp.int32, (chunk, chunk), 0)
    c_i = lax.broadcasted_iota(jnp.int32, (chunk, chunk), 1)
    lane_c = lax.broadcasted_iota(jnp.int32, (chunk, LANES), 1)
    n_groups = ws_ref.shape[0]
    gdim = gvb.shape[1] // n_groups
    per_blk = LANES // gdim
    ws = [jnp.where(c_i <= r_i, ws_ref[g], jnp.zeros((), BF16)) for g in range(n_groups)]
    for c in range(tm // chunk):
        rows = slice(c * chunk, (c + 1) * chunk)
        for jb in range(gvb.shape[1] // LANES):
            cols = slice(jb * LANES, (jb + 1) * LANES)
            blk = gvb[rows, cols]
            sp = _dot(ws[jb * per_blk], blk)
            for gi in range(1, per_blk):
                sp = jnp.where(lane_c >= gi * gdim, _dot(ws[jb * per_blk + gi], blk), sp)
            mg_ref[rows, cols] = (u[rows, cols] * (sp + bs_ref[:, cols])).astype(BF16)


def _mixer_sample_kernel(x_ref, g_ref, wqkv_ref, wf_ref, bf_ref, wugv_ref, qg_ref, kg_ref, gvg_ref,
                         bd_ref, w00_ref, b0_ref,
                         q_ref, k_ref, v_ref, lf_ref, gv_ref, mg_ref):
    q, k, v, logf, u, gv = _mixer_common(x_ref, g_ref, wqkv_ref, wf_ref, bf_ref, wugv_ref,
                                         qg_ref, kg_ref, gvg_ref, bd_ref)
    q_ref[...] = q
    k_ref[...] = k
    v_ref[...] = v
    lf_ref[...] = logf
    gv_ref[...] = gv
    mg_ref[...] = (u * (gv * w00_ref[...] + b0_ref[...])).astype(BF16)


def _attn_kernel(qp_ref, qa_ref, kp_ref, ka_ref, vp_ref, o_ref, m_ref, l_ref, acc_ref, *, tq, hd):
    i = pl.program_id(2)
    n_sub = LANES // hd
    q = jnp.concatenate([qp_ref[0], qa_ref[0]], axis=-1)
    lane_q = lax.broadcasted_iota(jnp.int32, q.shape, 1) & (LANES - 1)
    qs = [jnp.where((lane_q >= h * hd) & (lane_q < (h + 1) * hd), q, jnp.zeros((), BF16))
          for h in range(n_sub)]
    m_ref[...] = jnp.full_like(m_ref, NEG_INF)
    l_ref[...] = jnp.zeros_like(l_ref)
    acc_ref[...] = jnp.zeros_like(acc_ref)

    def step(j, masked):
        start = pl.multiple_of(j * tq, tq)
        kc = jnp.concatenate([kp_ref[0, pl.ds(start, tq), :], ka_ref[0, pl.ds(start, tq), :]], axis=-1)
        vc = vp_ref[0, pl.ds(start, tq), :]
        for h in range(n_sub):
            s = _dot_nt(qs[h], kc)
            if masked:
                r_i = lax.broadcasted_iota(jnp.int32, s.shape, 0)
                c_i = lax.broadcasted_iota(jnp.int32, s.shape, 1)
                s = jnp.where(c_i <= r_i, s, NEG_INF)
            m_prev = m_ref[h]
            m_next = jnp.maximum(m_prev, jnp.max(s, axis=-1, keepdims=True))
            alpha = jnp.exp2(m_prev - m_next)
            p = jnp.exp2(s - jnp.tile(m_next, (1, tq // LANES)))
            l_ref[h] = alpha * l_ref[h] + jnp.sum(p, axis=-1, keepdims=True)
            acc_ref[h] = alpha * acc_ref[h] + _dot(p.astype(BF16), vc)
            m_ref[h] = m_next

    def body(j, c):
        step(j, False)
        return c

    lax.fori_loop(0, i, body, 0)
    step(i, True)

    lane_o = lax.broadcasted_iota(jnp.int32, (tq, LANES), 1)
    out = acc_ref[0] / l_ref[0]
    for h in range(1, n_sub):
        out = jnp.where(lane_o >= h * hd, acc_ref[h] / l_ref[h], out)
    o_ref[0] = out.astype(o_ref.dtype)


def _attn_call(qp, qa, kp, ka, vp, tq, hd):
    b, s, w = qp.shape
    n_sub = LANES // hd
    q_spec = pl.BlockSpec((1, tq, LANES), lambda bi, hp, i: (bi, i, hp))
    kv_spec = pl.BlockSpec((1, s, LANES), lambda bi, hp, i: (bi, 0, hp))
    return pl.pallas_call(
        functools.partial(_attn_kernel, tq=tq, hd=hd),
        out_shape=jax.ShapeDtypeStruct((b, s, w), BF16),
        grid=(b, w // LANES, s // tq),
        in_specs=[q_spec, q_spec, kv_spec, kv_spec, kv_spec],
        out_specs=q_spec,
        scratch_shapes=[pltpu.VMEM((n_sub, tq, LANES), F32), pltpu.VMEM((n_sub, tq, LANES), F32),
                        pltpu.VMEM((n_sub, tq, LANES), F32)],
        compiler_params=_params(3),
        name="fox_attn_prompt",
    )(qp, qa, kp, ka, vp)


def _decode_kernel(pt_ref, qcol_ref, q_ref, kn_ref, vn_ref, fn_ref, ltri_ref, *refs,
                   g_pages, page, n_heads, hd):
    k_refs = refs[:g_pages]
    v_refs = refs[g_pages:2 * g_pages]
    lf_refs = refs[2 * g_pages:3 * g_pages]
    o_ref, m_ref, l_ref, acc_ref, car_ref = refs[3 * g_pages:]
    g = pl.program_id(1)
    width = n_heads * hd

    @pl.when(g == 0)
    def _():
        m_ref[...] = jnp.full_like(m_ref, NEG_INF)
        l_ref[...] = jnp.zeros_like(l_ref)
        acc_ref[...] = jnp.zeros_like(acc_ref)
        car_ref[...] = jnp.broadcast_to(fn_ref[0], car_ref.shape)

    lf = jnp.concatenate([r[0] for r in lf_refs], axis=0)
    pieces = jnp.concatenate(_split3(lf), axis=0)
    both = _dot(pieces, ltri_ref[...])
    gh = g_pages * n_heads
    both = both[:gh] + both[gh:2 * gh] + both[2 * gh:]
    inner = both[:, :page]
    total = both[:, page:]
    carry = car_ref[...]
    bias = [None] * g_pages
    for r in reversed(range(g_pages)):
        bias[r] = inner[r * n_heads:(r + 1) * n_heads] + carry
        carry = carry + total[r * n_heads:(r + 1) * n_heads]
    car_ref[...] = carry

    s_rows = [[None] * n_heads for _ in range(g_pages)]
    for h in range(n_heads):
        hs = slice(h * hd, (h + 1) * hd)
        qh = qcol_ref[0, hs, :]
        for r in range(g_pages):
            s_rows[r][h] = jnp.sum(k_refs[r][0, hs, :] * qh, axis=0, keepdims=True)
    s = [jnp.concatenate(s_rows[r], axis=0) + bias[r] * LOG2E for r in range(g_pages)]
    m_prev = m_ref[...]
    m_next = m_prev
    for r in range(g_pages):
        m_next = jnp.maximum(m_next, s[r])
    m_next = jnp.broadcast_to(jnp.max(m_next, axis=-1, keepdims=True), m_prev.shape)
    alpha = jnp.exp2(m_prev - m_next)
    p = [jnp.exp2(s[r] - m_next) for r in range(g_pages)]
    l_ref[...] = alpha * l_ref[...] + sum(p[1:], p[0])
    for h in range(n_heads):
        hs = slice(h * hd, (h + 1) * hd)
        pv = v_refs[0][0, hs, :] * p[0][h:h + 1, :]
        for r in range(1, g_pages):
            pv = pv + v_refs[r][0, hs, :] * p[r][h:h + 1, :]
        acc_ref[hs, :] = alpha[h:h + 1, :] * acc_ref[hs, :] + pv
    m_ref[...] = m_next

    @pl.when(g == pl.num_programs(1) - 1)
    def _():
        row = lax.broadcasted_iota(jnp.int32, (n_heads, width), 0)
        col = lax.broadcasted_iota(jnp.int32, (n_heads, width), 1)
        head_mask = (col >= row * hd) & (col < (row + 1) * hd)

        def head_row(x):
            return jnp.sum(jnp.where(head_mask, jnp.broadcast_to(x, (n_heads, width)), 0.0),
                           axis=0, keepdims=True)

        qbd = jnp.where(head_mask, jnp.broadcast_to(q_ref[0], (n_heads, width)), 0.0)
        s_new = jnp.sum(qbd * kn_ref[0], axis=-1, keepdims=True)
        m_old = m_ref[:, :1]
        m_fin = jnp.maximum(m_old, s_new)
        a_fin = jnp.exp2(m_old - m_fin)
        p_new = jnp.exp2(s_new - m_fin)
        l_fin = a_fin * jnp.sum(l_ref[...], axis=-1, keepdims=True) + p_new
        past = jnp.sum(acc_ref[...].T, axis=0, keepdims=True)
        out = (head_row(a_fin) * past + head_row(p_new) * vn_ref[0]) / head_row(l_fin)
        o_ref[0] = out.astype(o_ref.dtype)


def _decode_call(page_table, qcol, q, kn, vn, fn, cache_kt, cache_vt, cache_lf_t, g_pages, n_heads, hd):
    db, n_pages = page_table.shape
    n_pool, width, page = cache_kt.shape
    n_groups = n_pages // g_pages
    ltri = np.concatenate([np.tril(np.ones((page, page), np.float32), -1),
                           np.ones((page, page), np.float32)], axis=1)
    ltri = jnp.asarray(ltri, BF16)

    def page_map(r):
        return lambda b, g, pt: (pt[b, (n_groups - 1 - g) * g_pages + r], 0, 0)

    row_spec = pl.BlockSpec((1, 1, width), lambda b, g, pt: (b, 0, 0))
    in_specs = ([pl.BlockSpec((1, width, page), lambda b, g, pt: (b, 0, 0)),
                 row_spec, row_spec, row_spec,
                 pl.BlockSpec((1, n_heads, page), lambda b, g, pt: (b, 0, 0)),
                 pl.BlockSpec(ltri.shape, lambda b, g, pt: (0, 0))]
                + [pl.BlockSpec((1, width, page), page_map(r)) for r in range(g_pages)]
                + [pl.BlockSpec((1, width, page), page_map(r)) for r in range(g_pages)]
                + [pl.BlockSpec((1, n_heads, page), page_map(r)) for r in range(g_pages)])
    return pl.pallas_call(
        functools.partial(_decode_kernel, g_pages=g_pages, page=page, n_heads=n_heads, hd=hd),
        out_shape=jax.ShapeDtypeStruct((db, 1, width), BF16),
        grid_spec=pltpu.PrefetchScalarGridSpec(
            num_scalar_prefetch=1,
            grid=(db, n_groups),
            in_specs=in_specs,
            out_specs=pl.BlockSpec((1, 1, width), lambda b, g, pt: (b, 0, 0)),
            scratch_shapes=[pltpu.VMEM((n_heads, page), F32), pltpu.VMEM((n_heads, page), F32),
                            pltpu.VMEM((width, page), F32), pltpu.VMEM((n_heads, page), F32)]),
        compiler_params=_params(2),
        name="fox_attn_sample",
    )(page_table, qcol, q, kn, vn, fn, ltri, *([cache_kt] * g_pages), *([cache_vt] * g_pages),
      *([cache_lf_t] * g_pages))


def _merge_kernel(x_ref, a_ref, mg_ref, g_ref, wga_ref, wgb_ref, wpa_ref, wpg_ref, wo_ref, o_ref):
    x = x_ref[...]
    h = _rms(x, g_ref[...]).astype(BF16)
    ga = jax.nn.sigmoid(_dot(h, wga_ref[...]))
    gb = jax.nn.sigmoid(_dot(h, wgb_ref[...]))
    merged = ga * _dot(a_ref[...], wpa_ref[...]) + gb * _dot(mg_ref[...], wpg_ref[...])
    o_ref[...] = x + _dot(merged.astype(BF16), wo_ref[...])


def _merge_call(x, a, mg, g, wga, wgb, wpa, wpg, wo, tm):
    n, d = x.shape
    w = a.shape[1]
    row = lambda cols: pl.BlockSpec((tm, cols), lambda i: (i, 0))
    return pl.pallas_call(
        _merge_kernel,
        out_shape=jax.ShapeDtypeStruct((n, d), F32),
        grid=(n // tm,),
        in_specs=[row(d), row(w), row(mg.shape[1])] + [_const_spec(t.shape) for t in (g, wga, wgb, wpa, wpg, wo)],
        out_specs=row(d),
        compiler_params=_params(1),
        name="merge_out",
    )(x, a, mg, g, wga, wgb, wpa, wpg, wo)


def _ple_kernel(x_ref, p_ref, g_ref, wg_ref, wp_ref, o_ref):
    x = x_ref[...]
    gate = jax.nn.sigmoid(_dot(_rms(x, g_ref[...]).astype(BF16), wg_ref[...]))
    o_ref[...] = x + gate * _dot(p_ref[...].astype(BF16), wp_ref[...])


def _ple_call(x, p, g, wg, wp, tm):
    n, d = x.shape
    row = lambda cols: pl.BlockSpec((tm, cols), lambda i: (i, 0))
    return pl.pallas_call(
        _ple_kernel,
        out_shape=jax.ShapeDtypeStruct((n, d), F32),
        grid=(n // tm,),
        in_specs=[row(d), row(p.shape[1])] + [_const_spec(t.shape) for t in (g, wg, wp)],
        out_specs=row(d),
        compiler_params=_params(1),
        name="ple",
    )(x, p, g, wg, wp)


def _block_diag_mean(width, group):
    idx = np.arange(width) // group
    return jnp.asarray((idx[:, None] == idx[None, :]).astype(np.float32) / group, BF16)


def _forget_selectors(n_heads, hd):
    width = n_heads * hd
    selq = np.zeros((3 * LANES, width), np.float32)
    selk = np.zeros((3 * LANES, width), np.float32)
    cq = np.zeros((1, width), np.float32)
    ck = np.zeros((1, width), np.float32)
    for h in range(n_heads):
        for p in range(3):
            selq[p * LANES + h, h * hd + p] = 1.0
            selk[p * LANES + h, h * hd + 3 + p] = -1.0
            cq[0, h * hd + 3 + p] = 1.0
            ck[0, h * hd + p] = 1.0
    return (jnp.asarray(selq, BF16), jnp.asarray(selk, BF16), jnp.asarray(cq), jnp.asarray(ck))


def kernel(x_prompt, x_sample, cache_k, cache_v, cache_logf, page_table, p_prompt, p_sample,
           ffn1_norm, ffn1_w_gu, ffn1_w_down, mix_norm, w_in, b_forget, q_norm, k_norm,
           gmlp_v_norm, w_spatial, b_spatial, w_proj_attn, w_proj_gmlp, w_out,
           ffn2_norm, ffn2_w_gu, ffn2_w_down, ple_norm, ple_w_gate, ple_w_proj):
    depth = ffn1_norm.shape[0]
    bsz, seq, d = x_prompt.shape
    db = x_sample.shape[0]
    assert x_sample.shape[1] == 1, "the sample group decodes one token per step"
    n_heads, hd = cache_k.shape[3], cache_k.shape[4]
    aw = n_heads * hd
    n_groups, chunk = w_spatial.shape[1], w_spatial.shape[2]
    gw = w_proj_gmlp.shape[1]
    gdim = gw // n_groups
    page = cache_k.shape[2]
    n_pool = cache_k.shape[1]
    assert hd == gdim and aw == gw and LANES % hd == 0 and n_heads <= LANES and page == LANES

    tm = 512
    tq = 512
    g_pages = 16
    scale = hd ** -0.5
    tri = jnp.asarray(np.tril(np.ones((tm, tm), np.float32)), BF16)
    bd = _block_diag_mean(aw, hd)
    selq, selk, cq, ck = _forget_selectors(n_heads, hd)

    xp = x_prompt.reshape(bsz * seq, d)
    xs = x_sample.reshape(db, d)
    outs = {k: [] for k in ("kp", "vp", "fp", "ks", "vs", "fs", "gs")}
    for li in range(depth):
        bf = lambda t: t[li].astype(BF16)
        row = lambda t: t[li].reshape(1, -1).astype(F32)
        tile_h = lambda t, mult=1.0: jnp.tile(t[li].astype(F32) * mult, n_heads).reshape(1, -1)
        w1gu, w1d, w2gu, w2d = bf(ffn1_w_gu), bf(ffn1_w_down), bf(ffn2_w_gu), bf(ffn2_w_down)
        wi = w_in[li]
        o = 0
        wqkv = wi[:, o:o + 3 * aw].astype(BF16); o += 3 * aw
        wf = jnp.pad(wi[:, o:o + n_heads], ((0, 0), (0, LANES - n_heads))).astype(BF16); o += n_heads
        wugv = wi[:, o:o + 2 * gw].astype(BF16); o += 2 * gw
        wga = wi[:, o:o + d].astype(BF16); o += d
        wgb = wi[:, o:o + d].astype(BF16)
        bfg = jnp.pad(b_forget[li].astype(F32), (0, LANES - n_heads)).reshape(1, LANES)
        qg, kg, gvg = tile_h(q_norm, scale * LOG2E), tile_h(k_norm), tile_h(gmlp_v_norm)
        ws = bf(w_spatial)
        bs = jnp.repeat(b_spatial[li].astype(F32).T, gdim, axis=1)
        w00 = jnp.repeat(w_spatial[li, :, 0, 0].astype(F32), gdim).reshape(1, gw)
        b0 = jnp.repeat(b_spatial[li, :, 0].astype(F32), gdim).reshape(1, gw)
        wpa, wpg, wo = bf(w_proj_attn), bf(w_proj_gmlp), bf(w_out)
        plg, plp = bf(ple_w_gate), bf(ple_w_proj)
        mixer_w = (row(mix_norm), wqkv, wf, bfg, wugv, qg, kg, gvg, bd)

        n = bsz * seq
        x1 = _ffn_call(xp, row(ffn1_norm), w1gu, w1d, tm)
        rowspec = lambda cols: pl.BlockSpec((tm, cols), lambda i: (i, 0))
        consts = mixer_w + (tri, selq, selk, cq, ck, ws, bs)
        sds = lambda cols, dt: jax.ShapeDtypeStruct((n, cols), dt)
        tps = seq // tm
        tsd = lambda rows_: jax.ShapeDtypeStruct((bsz, rows_, seq), F32)
        tspec = lambda rows_: pl.BlockSpec((1, rows_, tm), lambda i: (i // tps, 0, i % tps))
        qp, qa, k32t, kp, ka, v32t, vp, lfpt, mg = pl.pallas_call(
            functools.partial(_mixer_prompt_kernel, tiles_per_seq=tps, n_heads=n_heads, chunk=chunk),
            out_shape=(sds(aw, BF16), sds(aw, BF16), tsd(aw), sds(aw, BF16), sds(aw, BF16),
                       tsd(aw), sds(aw, BF16), tsd(n_heads), sds(gw, BF16)),
            grid=(n // tm,),
            in_specs=[rowspec(d)] + [_const_spec(t.shape) for t in consts],
            out_specs=(rowspec(aw), rowspec(aw), tspec(aw), rowspec(aw), rowspec(aw),
                       tspec(aw), rowspec(aw), tspec(n_heads), rowspec(gw)),
            scratch_shapes=[pltpu.VMEM((1, LANES), F32)],
            compiler_params=_params(1),
            name="mixer_prompt",
        )(x1, *consts)
        r3 = lambda t: t.reshape(bsz, seq, aw)
        a = _attn_call(r3(qp), r3(qa), r3(kp), r3(ka), r3(vp), tq, hd).reshape(n, aw)
        x2 = _merge_call(x1, a, mg, row(mix_norm), wga, wgb, wpa, wpg, wo, tm)
        x3 = _ffn_call(x2, row(ffn2_norm), w2gu, w2d, tm)
        xp = _ple_call(x3, p_prompt[li].reshape(n, -1), row(ple_norm), plg, plp, tm)
        outs["kp"].append(k32t.reshape(bsz, n_heads, hd, seq).transpose(0, 3, 1, 2))
        outs["vp"].append(v32t.reshape(bsz, n_heads, hd, seq).transpose(0, 3, 1, 2))
        outs["fp"].append(lfpt.transpose(0, 2, 1))

        x1s = _ffn_call(xs, row(ffn1_norm), w1gu, w1d, db)
        sconsts = mixer_w + (w00, b0)
        full = lambda cols: pl.BlockSpec((db, cols), lambda i: (0, 0))
        ssd = lambda cols, dt: jax.ShapeDtypeStruct((db, cols), dt)
        qs, ksn, vsn, lfs, gvs, mgs = pl.pallas_call(
            _mixer_sample_kernel,
            out_shape=(ssd(aw, F32), ssd(aw, F32), ssd(aw, F32), ssd(LANES, F32), ssd(gw, F32), ssd(gw, BF16)),
            grid=(1,),
            in_specs=[full(d)] + [_const_spec(t.shape) for t in sconsts],
            out_specs=tuple(full(c) for c in (aw, aw, aw, LANES, gw, gw)),
            compiler_params=_params(1),
            name="mixer_sample",
        )(x1s, *sconsts)
        fn = jnp.broadcast_to(lfs[:, :n_heads, None], (db, n_heads, page))
        qcol = jnp.broadcast_to(qs[:, :, None], (db, aw, page))
        cache_lf_t = jnp.swapaxes(cache_logf[li].astype(F32), 1, 2)
        cache_kt = jnp.transpose(cache_k[li], (0, 2, 3, 1)).reshape(n_pool, aw, page)
        cache_vt = jnp.transpose(cache_v[li], (0, 2, 3, 1)).reshape(n_pool, aw, page)
        a_s = _decode_call(page_table, qcol, qs.reshape(db, 1, aw), ksn.reshape(db, 1, aw),
                           vsn.reshape(db, 1, aw), fn, cache_kt, cache_vt,
                           cache_lf_t, g_pages, n_heads, hd).reshape(db, aw)
        x2s = _merge_call(x1s, a_s, mgs, row(mix_norm), wga, wgb, wpa, wpg, wo, db)
        x3s = _ffn_call(x2s, row(ffn2_norm), w2gu, w2d, db)
        xs = _ple_call(x3s, p_sample[li].reshape(db, -1), row(ple_norm), plg, plp, db)
        outs["ks"].append(ksn.reshape(db, 1, n_heads, hd))
        outs["vs"].append(vsn.reshape(db, 1, n_heads, hd))
        outs["fs"].append(lfs[:, :n_heads].reshape(db, 1, n_heads))
        outs["gs"].append(gvs.reshape(db, 1, n_groups, gdim))

    st = lambda key: jnp.stack(outs[key])
    return (xp.reshape(bsz, seq, d), xs.reshape(db, 1, d), st("kp"), st("vp"), st("fp"),
            st("ks"), st("vs"), st("fs"), st("gs"))
```

```python
import functools

import jax
import jax.numpy as jnp
import numpy as np
from jax import lax
from jax.experimental import pallas as pl
from jax.experimental.pallas import tpu as pltpu

EPS = 1e-6
NEG_INF = -1e30
LOG2E = 1.4426950408889634
LANES = 128
BF16_ROWS = 16
F32 = jnp.float32
BF16 = jnp.bfloat16
VMEM_LIMIT = 56 * 1024 * 1024


def _dot(a, b):
    return jnp.dot(a, b, preferred_element_type=F32)


def _dot_nt(a, b):
    return lax.dot_general(a, b, (((1,), (1,)), ((), ())), preferred_element_type=F32)


def _rms(x, g):
    ms = jnp.mean(x * x, axis=-1, keepdims=True)
    return x * lax.rsqrt(ms + EPS) * g


def _group_rms(x, g, bd):
    ms = _dot((x * x).astype(BF16), bd)
    return x * lax.rsqrt(ms + EPS) * g


def _split3(x):
    hi = x.astype(BF16)
    r = x - hi.astype(F32)
    mid = r.astype(BF16)
    lo = (r - mid.astype(F32)).astype(BF16)
    return hi, mid, lo


def _const_spec(shape):
    nd = len(shape)
    return pl.BlockSpec(shape, lambda *_: (0,) * nd, pipeline_mode=pl.Buffered(1))


def _params(n_axes):
    return pltpu.CompilerParams(dimension_semantics=("arbitrary",) * n_axes,
                                vmem_limit_bytes=VMEM_LIMIT)


def _ffn_kernel(x_ref, g_ref, wgu_ref, wd_ref, o_ref, act_ref, *, d_ff, chunk):
    x = x_ref[...]
    h = _rms(x, g_ref[...]).astype(BF16)
    for c in range(d_ff // chunk):
        a = _dot(h, wgu_ref[:, c * chunk:(c + 1) * chunk])
        b = _dot(h, wgu_ref[:, d_ff + c * chunk:d_ff + (c + 1) * chunk])
        act_ref[:, c * chunk:(c + 1) * chunk] = (jax.nn.silu(a) * b).astype(BF16)
    o_ref[...] = x + 0.5 * _dot(act_ref[...], wd_ref[...])


def _ffn_call(x, g, wgu, wd, tm):
    n, d = x.shape
    d_ff = wd.shape[0]
    chunk = 256 if d_ff % 256 == 0 else LANES
    return pl.pallas_call(
        functools.partial(_ffn_kernel, d_ff=d_ff, chunk=chunk),
        out_shape=jax.ShapeDtypeStruct((n, d), F32),
        grid=(n // tm,),
        in_specs=[pl.BlockSpec((tm, d), lambda i: (i, 0)),
                  _const_spec(g.shape), _const_spec(wgu.shape), _const_spec(wd.shape)],
        out_specs=pl.BlockSpec((tm, d), lambda i: (i, 0)),
        scratch_shapes=[pltpu.VMEM((tm, d_ff), BF16)],
        compiler_params=_params(1),
        name="ffn",
    )(x, g, wgu, wd)


def _mixer_common(x_ref, g_ref, wqkv_ref, wf_ref, bf_ref, wugv_ref, qg_ref, kg_ref, gvg_ref, bd_ref):
    h = _rms(x_ref[...], g_ref[...]).astype(BF16)
    aw = qg_ref.shape[1]
    qkv = _dot(h, wqkv_ref[...])
    q = _group_rms(qkv[:, :aw], qg_ref[...], bd_ref[...])
    k = _group_rms(qkv[:, aw:2 * aw], kg_ref[...], bd_ref[...])
    v = qkv[:, 2 * aw:]
    logf = jax.nn.log_sigmoid(_dot(h, wf_ref[...]) + bf_ref[...])
    ugv = _dot(h, wugv_ref[...])
    gw = gvg_ref.shape[1]
    u = jax.nn.gelu(ugv[:, :gw])
    gv = _group_rms(jax.nn.gelu(ugv[:, gw:]), gvg_ref[...], bd_ref[...])
    return q, k, v, logf, u, gv


def _mixer_prompt_kernel(x_ref, g_ref, wqkv_ref, wf_ref, bf_ref, wugv_ref, qg_ref, kg_ref, gvg_ref,
                         bd_ref, tri_ref, selqt_ref, selk_ref, ws_ref, bs_ref,
                         qt_ref, qat_ref, k_ref, kp_ref, ka_ref, v_ref, vt_ref, lf_ref, mg_ref,
                         carry_ref, *, tiles_per_seq, n_heads, chunk):
    q, k, v, logf, u, gv = _mixer_common(x_ref, g_ref, wqkv_ref, wf_ref, bf_ref, wugv_ref,
                                         qg_ref, kg_ref, gvg_ref, bd_ref)
    tm = q.shape[0]
    qt_ref[0] = q.T.astype(BF16)
    k_ref[0] = k.T
    kp_ref[...] = k.astype(BF16)
    v_t = v.T
    v_ref[0] = v_t
    vt_ref[0] = v_t.astype(BF16)
    lf_ref[0] = logf.T[:n_heads]

    @pl.when(pl.program_id(0) % tiles_per_seq == 0)
    def _():
        carry_ref[...] = jnp.zeros_like(carry_ref)

    lane = lax.broadcasted_iota(jnp.int32, logf.shape, 1)
    lf = jnp.where(lane < n_heads, logf, 0.0)
    pieces = jnp.concatenate(_split3(lf), axis=-1)
    cs = _dot(tri_ref[...], pieces)
    f_cum = cs[:, :LANES] + cs[:, LANES:2 * LANES] + cs[:, 2 * LANES:] + carry_ref[...]
    carry_ref[...] = f_cum[tm - 1:tm, :]
    f_ext = jnp.where(lane == n_heads, 1.0, f_cum * LOG2E)
    ka_ref[...] = _dot(jnp.concatenate(_split3(f_ext), axis=-1), selk_ref[...]).astype(BF16)
    fpt = jnp.concatenate(_split3(f_ext.T), axis=0)
    qat_ref[0] = _dot(selqt_ref[...], fpt).astype(BF16)

    gvb = gv.astype(BF16)
    r_i = lax.broadcasted_iota(jnp.int32, (chunk, chunk), 0)
    c_i = lax.broadcasted_iota(jnp.int32, (chunk, chunk), 1)
    lane_c = lax.broadcasted_iota(jnp.int32, (chunk, LANES), 1)
    n_groups = ws_ref.shape[0]
    gdim = gvb.shape[1] // n_groups
    per_blk = LANES // gdim
    ws = [jnp.where(c_i <= r_i, ws_ref[g], jnp.zeros((), BF16)) for g in range(n_groups)]
    for c in range(tm // chunk):
        rows = slice(c * chunk, (c + 1) * chunk)
        for jb in range(gvb.shape[1] // LANES):
            cols = slice(jb * LANES, (jb + 1) * LANES)
            blk = gvb[rows, cols]
            sp = _dot(ws[jb * per_blk], blk)
            for gi in range(1, per_blk):
                sp = jnp.where(lane_c >= gi * gdim, _dot(ws[jb * per_blk + gi], blk), sp)
            mg_ref[rows, cols] = (u[rows, cols] * (sp + bs_ref[:, cols])).astype(BF16)


def _mixer_sample_kernel(x_ref, g_ref, wqkv_ref, wf_ref, bf_ref, wugv_ref, qg_ref, kg_ref, gvg_ref,
                         bd_ref, w00_ref, b0_ref,
                         q_ref, k_ref, v_ref, lf_ref, gv_ref, mg_ref):
    q, k, v, logf, u, gv = _mixer_common(x_ref, g_ref, wqkv_ref, wf_ref, bf_ref, wugv_ref,
                                         qg_ref, kg_ref, gvg_ref, bd_ref)
    q_ref[...] = q
    k_ref[...] = k
    v_ref[...] = v
    lf_ref[...] = logf
    gv_ref[...] = gv
    mg_ref[...] = (u * (gv * w00_ref[...] + b0_ref[...])).astype(BF16)


def _attn_kernel(qt_ref, qat_ref, kp_ref, ka_ref, vt_ref, o_ref, m_ref, acc_ref,
                 s_ref, p_ref, al_ref, *, tq, hd):
    i = pl.program_id(2)
    n_sub = LANES // hd
    qcat = jnp.concatenate([qt_ref[0], qat_ref[0]], axis=0)
    row_q = lax.broadcasted_iota(jnp.int32, qcat.shape, 0) & (LANES - 1)
    qs = [jnp.where((row_q >= h * hd) & (row_q < (h + 1) * hd), qcat, jnp.zeros((), BF16))
          for h in range(n_sub)]
    m_ref[...] = jnp.full_like(m_ref, NEG_INF)
    acc_ref[...] = jnp.zeros_like(acc_ref)
    p_ref[1] = jnp.zeros_like(p_ref[1])
    al_ref[1] = jnp.ones_like(al_ref[1])

    def scores(t, slot):
        start = pl.multiple_of(t * tq, tq)
        kc = jnp.concatenate([kp_ref[0, pl.ds(start, tq), :], ka_ref[0, pl.ds(start, tq), :]], axis=-1)
        for h in range(n_sub):
            s_ref[slot, h] = _dot(kc, qs[h])

    def softmax(slot, masked):
        for h in range(n_sub):
            s = s_ref[slot, h]
            if masked:
                key = lax.broadcasted_iota(jnp.int32, s.shape, 0)
                qry = lax.broadcasted_iota(jnp.int32, s.shape, 1)
                s = jnp.where(key <= qry, s, NEG_INF)
            m_prev = m_ref[h]
            m_next = jnp.maximum(m_prev, jnp.max(s, axis=0, keepdims=True))
            alpha = jnp.exp2(m_prev - m_next)
            p = jnp.exp2(s - m_next)
            m_ref[h] = m_next
            al_ref[slot, h] = alpha
            p_ref[slot, h] = p.astype(BF16)

    def values(t, slot):
        start = pl.multiple_of(jnp.maximum(t, 0) * tq, tq)
        for h in range(n_sub):
            v_t = jnp.concatenate([vt_ref[0, h * hd:(h + 1) * hd, pl.ds(start, tq)],
                                   jnp.ones((BF16_ROWS, tq), BF16)], axis=0)
            acc_ref[h] = al_ref[slot, h] * acc_ref[h] + _dot(v_t, p_ref[slot, h])

    def stage(t, slot):
        scores(t + 1, 1 - slot)
        softmax(slot, False)
        values(t - 1, 1 - slot)

    def body(u, c):
        stage(2 * u, 0)
        stage(2 * u + 1, 1)
        return c

    scores(0, 0)
    lax.fori_loop(0, i // 2, body, 0)

    @pl.when(i % 2 == 1)
    def _():
        stage(i - 1, 0)
        softmax(1, True)
        values(i - 1, 0)
        values(i, 1)

    @pl.when(i % 2 == 0)
    def _():
        softmax(0, True)
        values(i - 1, 1)
        values(i, 0)

    out_t = jnp.concatenate([acc_ref[h, :hd] / acc_ref[h, hd:hd + 1] for h in range(n_sub)], axis=0)
    o_ref[0] = out_t.T.astype(o_ref.dtype)


def _attn_call(qt, qat, kp, ka, vt, tq, hd):
    b, s, w = kp.shape
    n_sub = LANES // hd
    qt_spec = pl.BlockSpec((1, LANES, tq), lambda bi, hp, i: (bi, hp, i))
    k_spec = pl.BlockSpec((1, s, LANES), lambda bi, hp, i: (bi, 0, hp))
    return pl.pallas_call(
        functools.partial(_attn_kernel, tq=tq, hd=hd),
        out_shape=jax.ShapeDtypeStruct((b, s, w), BF16),
        grid=(b, w // LANES, s // tq),
        in_specs=[qt_spec, qt_spec, k_spec, k_spec,
                  pl.BlockSpec((1, LANES, s), lambda bi, hp, i: (bi, hp, 0))],
        out_specs=pl.BlockSpec((1, tq, LANES), lambda bi, hp, i: (bi, i, hp)),
        scratch_shapes=[pltpu.VMEM((n_sub, 1, tq), F32),
                        pltpu.VMEM((n_sub, hd + BF16_ROWS, tq), F32), pltpu.VMEM((2, n_sub, tq, tq), F32),
                        pltpu.VMEM((2, n_sub, tq, tq), BF16), pltpu.VMEM((2, n_sub, 1, tq), F32)],
        compiler_params=_params(3),
        name="fox_attn_prompt",
    )(qt, qat, kp, ka, vt)


def _decode_kernel(pt_ref, qcol_ref, q_ref, kn_ref, vn_ref, fn_ref, ltri_ref, *refs,
                   g_pages, page, n_heads, hd):
    k_refs = refs[:g_pages]
    v_refs = refs[g_pages:2 * g_pages]
    lf_refs = refs[2 * g_pages:3 * g_pages]
    o_ref, m_ref, l_ref, acc_ref, car_ref = refs[3 * g_pages:]
    g = pl.program_id(1)
    width = n_heads * hd

    @pl.when(g == 0)
    def _():
        m_ref[...] = jnp.full_like(m_ref, NEG_INF)
        l_ref[...] = jnp.zeros_like(l_ref)
        acc_ref[...] = jnp.zeros_like(acc_ref)
        car_ref[...] = jnp.broadcast_to(fn_ref[0], car_ref.shape)

    lf = jnp.concatenate([r[0] for r in lf_refs], axis=0)
    pieces = jnp.concatenate(_split3(lf), axis=0)
    both = _dot(pieces, ltri_ref[...])
    gh = g_pages * n_heads
    both = both[:gh] + both[gh:2 * gh] + both[2 * gh:]
    inner = both[:, :page]
    total = both[:, page:]
    carry = car_ref[...]
    bias = [None] * g_pages
    for r in reversed(range(g_pages)):
        bias[r] = inner[r * n_heads:(r + 1) * n_heads] + carry
        carry = carry + total[r * n_heads:(r + 1) * n_heads]
    car_ref[...] = carry

    s_rows = [[None] * n_heads for _ in range(g_pages)]
    for h in range(n_heads):
        hs = slice(h * hd, (h + 1) * hd)
        qh = qcol_ref[0, hs, :]
        for r in range(g_pages):
            s_rows[r][h] = jnp.sum(k_refs[r][0, hs, :] * qh, axis=0, keepdims=True)
    s = [jnp.concatenate(s_rows[r], axis=0) + bias[r] * LOG2E for r in range(g_pages)]
    m_prev = m_ref[...]
    m_next = m_prev
    for r in range(g_pages):
        m_next = jnp.maximum(m_next, s[r])
    m_next = jnp.broadcast_to(jnp.max(m_next, axis=-1, keepdims=True), m_prev.shape)
    alpha = jnp.exp2(m_prev - m_next)
    p = [jnp.exp2(s[r] - m_next) for r in range(g_pages)]
    l_ref[...] = alpha * l_ref[...] + sum(p[1:], p[0])
    for h in range(n_heads):
        hs = slice(h * hd, (h + 1) * hd)
        pv = v_refs[0][0, hs, :] * p[0][h:h + 1, :]
        for r in range(1, g_pages):
            pv = pv + v_refs[r][0, hs, :] * p[r][h:h + 1, :]
        acc_ref[hs, :] = alpha[h:h + 1, :] * acc_ref[hs, :] + pv
    m_ref[...] = m_next

    @pl.when(g == pl.num_programs(1) - 1)
    def _():
        row = lax.broadcasted_iota(jnp.int32, (n_heads, width), 0)
        col = lax.broadcasted_iota(jnp.int32, (n_heads, width), 1)
        head_mask = (col >= row * hd) & (col < (row + 1) * hd)

        def head_row(x):
            return jnp.sum(jnp.where(head_mask, jnp.broadcast_to(x, (n_heads, width)), 0.0),
                           axis=0, keepdims=True)

        qbd = jnp.where(head_mask, jnp.broadcast_to(q_ref[0], (n_heads, width)), 0.0)
        s_new = jnp.sum(qbd * kn_ref[0], axis=-1, keepdims=True)
        m_old = m_ref[:, :1]
        m_fin = jnp.maximum(m_old, s_new)
        a_fin = jnp.exp2(m_old - m_fin)
        p_new = jnp.exp2(s_new - m_fin)
        l_fin = a_fin * jnp.sum(l_ref[...], axis=-1, keepdims=True) + p_new
        past = jnp.sum(acc_ref[...].T, axis=0, keepdims=True)
        out = (head_row(a_fin) * past + head_row(p_new) * vn_ref[0]) / head_row(l_fin)
        o_ref[0] = out.astype(o_ref.dtype)


def _decode_call(page_table, qcol, q, kn, vn, fn, cache_kt, cache_vt, cache_lf_t, g_pages, n_heads, hd):
    db, n_pages = page_table.shape
    n_pool, width, page = cache_kt.shape
    n_groups = n_pages // g_pages
    ltri = np.concatenate([np.tril(np.ones((page, page), np.float32), -1),
                           np.ones((page, page), np.float32)], axis=1)
    ltri = jnp.asarray(ltri, BF16)

    def page_map(r):
        return lambda b, g, pt: (pt[b, (n_groups - 1 - g) * g_pages + r], 0, 0)

    row_spec = pl.BlockSpec((1, 1, width), lambda b, g, pt: (b, 0, 0))
    in_specs = ([pl.BlockSpec((1, width, page), lambda b, g, pt: (b, 0, 0)),
                 row_spec, row_spec, row_spec,
                 pl.BlockSpec((1, n_heads, page), lambda b, g, pt: (b, 0, 0)),
                 pl.BlockSpec(ltri.shape, lambda b, g, pt: (0, 0))]
                + [pl.BlockSpec((1, width, page), page_map(r)) for r in range(g_pages)]
                + [pl.BlockSpec((1, width, page), page_map(r)) for r in range(g_pages)]
                + [pl.BlockSpec((1, n_heads, page), page_map(r)) for r in range(g_pages)])
    return pl.pallas_call(
        functools.partial(_decode_kernel, g_pages=g_pages, page=page, n_heads=n_heads, hd=hd),
        out_shape=jax.ShapeDtypeStruct((db, 1, width), BF16),
        grid_spec=pltpu.PrefetchScalarGridSpec(
            num_scalar_prefetch=1,
            grid=(db, n_groups),
            in_specs=in_specs,
            out_specs=pl.BlockSpec((1, 1, width), lambda b, g, pt: (b, 0, 0)),
            scratch_shapes=[pltpu.VMEM((n_heads, page), F32), pltpu.VMEM((n_heads, page), F32),
                            pltpu.VMEM((width, page), F32), pltpu.VMEM((n_heads, page), F32)]),
        compiler_params=_params(2),
        name="fox_attn_sample",
    )(page_table, qcol, q, kn, vn, fn, ltri, *([cache_kt] * g_pages), *([cache_vt] * g_pages),
      *([cache_lf_t] * g_pages))


def _merge_kernel(x_ref, a_ref, mg_ref, g_ref, wga_ref, wgb_ref, wpa_ref, wpg_ref, wo_ref, o_ref):
    x = x_ref[...]
    h = _rms(x, g_ref[...]).astype(BF16)
    ga = jax.nn.sigmoid(_dot(h, wga_ref[...]))
    gb = jax.nn.sigmoid(_dot(h, wgb_ref[...]))
    merged = ga * _dot(a_ref[...], wpa_ref[...]) + gb * _dot(mg_ref[...], wpg_ref[...])
    o_ref[...] = x + _dot(merged.astype(BF16), wo_ref[...])


def _merge_call(x, a, mg, g, wga, wgb, wpa, wpg, wo, tm):
    n, d = x.shape
    w = a.shape[1]
    row = lambda cols: pl.BlockSpec((tm, cols), lambda i: (i, 0))
    return pl.pallas_call(
        _merge_kernel,
        out_shape=jax.ShapeDtypeStruct((n, d), F32),
        grid=(n // tm,),
        in_specs=[row(d), row(w), row(mg.shape[1])] + [_const_spec(t.shape) for t in (g, wga, wgb, wpa, wpg, wo)],
        out_specs=row(d),
        compiler_params=_params(1),
        name="merge_out",
    )(x, a, mg, g, wga, wgb, wpa, wpg, wo)


def _ple_kernel(x_ref, p_ref, g_ref, wg_ref, wp_ref, o_ref):
    x = x_ref[...]
    gate = jax.nn.sigmoid(_dot(_rms(x, g_ref[...]).astype(BF16), wg_ref[...]))
    o_ref[...] = x + gate * _dot(p_ref[...].astype(BF16), wp_ref[...])


def _ple_call(x, p, g, wg, wp, tm):
    n, d = x.shape
    row = lambda cols: pl.BlockSpec((tm, cols), lambda i: (i, 0))
    return pl.pallas_call(
        _ple_kernel,
        out_shape=jax.ShapeDtypeStruct((n, d), F32),
        grid=(n // tm,),
        in_specs=[row(d), row(p.shape[1])] + [_const_spec(t.shape) for t in (g, wg, wp)],
        out_specs=row(d),
        compiler_params=_params(1),
        name="ple",
    )(x, p, g, wg, wp)


def _block_diag_mean(width, group):
    idx = np.arange(width) // group
    return jnp.asarray((idx[:, None] == idx[None, :]).astype(np.float32) / group, BF16)


def _forget_selectors(n_heads, hd):
    width = n_heads * hd
    selq = np.zeros((3 * LANES, width), np.float32)
    selk = np.zeros((3 * LANES, width), np.float32)
    for h in range(n_heads):
        for p in range(3):
            selq[p * LANES + h, h * hd + p] = 1.0
            selk[p * LANES + h, h * hd + 3 + p] = -1.0
            selq[n_heads, h * hd + 3 + p] = 1.0
            selk[n_heads, h * hd + p] = 1.0
    return jnp.asarray(selq.T, BF16), jnp.asarray(selk, BF16)


def kernel(x_prompt, x_sample, cache_k, cache_v, cache_logf, page_table, p_prompt, p_sample,
           ffn1_norm, ffn1_w_gu, ffn1_w_down, mix_norm, w_in, b_forget, q_norm, k_norm,
           gmlp_v_norm, w_spatial, b_spatial, w_proj_attn, w_proj_gmlp, w_out,
           ffn2_norm, ffn2_w_gu, ffn2_w_down, ple_norm, ple_w_gate, ple_w_proj):
    depth = ffn1_norm.shape[0]
    bsz, seq, d = x_prompt.shape
    db = x_sample.shape[0]
    assert x_sample.shape[1] == 1, "the sample group decodes one token per step"
    n_heads, hd = cache_k.shape[3], cache_k.shape[4]
    aw = n_heads * hd
    n_groups, chunk = w_spatial.shape[1], w_spatial.shape[2]
    gw = w_proj_gmlp.shape[1]
    gdim = gw // n_groups
    page = cache_k.shape[2]
    n_pool = cache_k.shape[1]
    assert hd == gdim and aw == gw and LANES % hd == 0 and n_heads <= LANES and page == LANES

    tm = 512
    tq = 512
    g_pages = 16
    scale = hd ** -0.5
    tri = jnp.asarray(np.tril(np.ones((tm, tm), np.float32)), BF16)
    bd = _block_diag_mean(aw, hd)
    selqt, selk = _forget_selectors(n_heads, hd)

    xp = x_prompt.reshape(bsz * seq, d)
    xs = x_sample.reshape(db, d)
    outs = {k: [] for k in ("kp", "vp", "fp", "ks", "vs", "fs", "gs")}
    for li in range(depth):
        bf = lambda t: t[li].astype(BF16)
        row = lambda t: t[li].reshape(1, -1).astype(F32)
        tile_h = lambda t, mult=1.0: jnp.tile(t[li].astype(F32) * mult, n_heads).reshape(1, -1)
        w1gu, w1d, w2gu, w2d = bf(ffn1_w_gu), bf(ffn1_w_down), bf(ffn2_w_gu), bf(ffn2_w_down)
        wi = w_in[li]
        o = 0
        wqkv = wi[:, o:o + 3 * aw].astype(BF16); o += 3 * aw
        wf = jnp.pad(wi[:, o:o + n_heads], ((0, 0), (0, LANES - n_heads))).astype(BF16); o += n_heads
        wugv = wi[:, o:o + 2 * gw].astype(BF16); o += 2 * gw
        wga = wi[:, o:o + d].astype(BF16); o += d
        wgb = wi[:, o:o + d].astype(BF16)
        bfg = jnp.pad(b_forget[li].astype(F32), (0, LANES - n_heads)).reshape(1, LANES)
        qg, kg, gvg = tile_h(q_norm, scale * LOG2E), tile_h(k_norm), tile_h(gmlp_v_norm)
        ws = bf(w_spatial)
        bs = jnp.repeat(b_spatial[li].astype(F32).T, gdim, axis=1)
        w00 = jnp.repeat(w_spatial[li, :, 0, 0].astype(F32), gdim).reshape(1, gw)
        b0 = jnp.repeat(b_spatial[li, :, 0].astype(F32), gdim).reshape(1, gw)
        wpa, wpg, wo = bf(w_proj_attn), bf(w_proj_gmlp), bf(w_out)
        plg, plp = bf(ple_w_gate), bf(ple_w_proj)
        mixer_w = (row(mix_norm), wqkv, wf, bfg, wugv, qg, kg, gvg, bd)

        n = bsz * seq
        x1 = _ffn_call(xp, row(ffn1_norm), w1gu, w1d, tm)
        rowspec = lambda cols: pl.BlockSpec((tm, cols), lambda i: (i, 0))
        consts = mixer_w + (tri, selqt, selk, ws, bs)
        sds = lambda cols, dt: jax.ShapeDtypeStruct((n, cols), dt)
        tps = seq // tm
        tsd = lambda rows_, dt: jax.ShapeDtypeStruct((bsz, rows_, seq), dt)
        tspec = lambda rows_: pl.BlockSpec((1, rows_, tm), lambda i: (i // tps, 0, i % tps))
        qt, qat, k32t, kp, ka, v32t, vt, lfpt, mg = pl.pallas_call(
            functools.partial(_mixer_prompt_kernel, tiles_per_seq=tps, n_heads=n_heads, chunk=chunk),
            out_shape=(tsd(aw, BF16), tsd(aw, BF16), tsd(aw, F32), sds(aw, BF16), sds(aw, BF16),
                       tsd(aw, F32), tsd(aw, BF16), tsd(n_heads, F32), sds(gw, BF16)),
            grid=(n // tm,),
            in_specs=[rowspec(d)] + [_const_spec(t.shape) for t in consts],
            out_specs=(tspec(aw), tspec(aw), tspec(aw), rowspec(aw), rowspec(aw),
                       tspec(aw), tspec(aw), tspec(n_heads), rowspec(gw)),
            scratch_shapes=[pltpu.VMEM((1, LANES), F32)],
            compiler_params=_params(1),
            name="mixer_prompt",
        )(x1, *consts)
        r3 = lambda t: t.reshape(bsz, seq, aw)
        a = _attn_call(qt, qat, r3(kp), r3(ka), vt, tq, hd).reshape(n, aw)
        x2 = _merge_call(x1, a, mg, row(mix_norm), wga, wgb, wpa, wpg, wo, tm)
        x3 = _ffn_call(x2, row(ffn2_norm), w2gu, w2d, tm)
        xp = _ple_call(x3, p_prompt[li].reshape(n, -1), row(ple_norm), plg, plp, tm)
        outs["kp"].append(k32t.reshape(bsz, n_heads, hd, seq).transpose(0, 3, 1, 2))
        outs["vp"].append(v32t.reshape(bsz, n_heads, hd, seq).transpose(0, 3, 1, 2))
        outs["fp"].append(lfpt.transpose(0, 2, 1))

        x1s = _ffn_call(xs, row(ffn1_norm), w1gu, w1d, db)
        sconsts = mixer_w + (w00, b0)
        full = lambda cols: pl.BlockSpec((db, cols), lambda i: (0, 0))
        ssd = lambda cols, dt: jax.ShapeDtypeStruct((db, cols), dt)
        qs, ksn, vsn, lfs, gvs, mgs = pl.pallas_call(
            _mixer_sample_kernel,
            out_shape=(ssd(aw, F32), ssd(aw, F32), ssd(aw, F32), ssd(LANES, F32), ssd(gw, F32), ssd(gw, BF16)),
            grid=(1,),
            in_specs=[full(d)] + [_const_spec(t.shape) for t in sconsts],
            out_specs=tuple(full(c) for c in (aw, aw, aw, LANES, gw, gw)),
            compiler_params=_params(1),
            name="mixer_sample",
        )(x1s, *sconsts)
        fn = jnp.broadcast_to(lfs[:, :n_heads, None], (db, n_heads, page))
        qcol = jnp.broadcast_to(qs[:, :, None], (db, aw, page))
        cache_lf_t = jnp.swapaxes(cache_logf[li].astype(F32), 1, 2)
        cache_kt = jnp.transpose(cache_k[li], (0, 2, 3, 1)).reshape(n_pool, aw, page)
        cache_vt = jnp.transpose(cache_v[li], (0, 2, 3, 1)).reshape(n_pool, aw, page)
        a_s = _decode_call(page_table, qcol, qs.reshape(db, 1, aw), ksn.reshape(db, 1, aw),
                           vsn.reshape(db, 1, aw), fn, cache_kt, cache_vt,
                           cache_lf_t, g_pages, n_heads, hd).reshape(db, aw)
        x2s = _merge_call(x1s, a_s, mgs, row(mix_norm), wga, wgb, wpa, wpg, wo, db)
        x3s = _ffn_call(x2s, row(ffn2_norm), w2gu, w2d, db)
        xs = _ple_call(x3s, p_sample[li].reshape(db, -1), row(ple_norm), plg, plp, db)
        outs["ks"].append(ksn.reshape(db, 1, n_heads, hd))
        outs["vs"].append(vsn.reshape(db, 1, n_heads, hd))
        outs["fs"].append(lfs[:, :n_heads].reshape(db, 1, n_heads))
        outs["gs"].append(gvs.reshape(db, 1, n_groups, gdim))

    st = lambda key: jnp.stack(outs[key])
    return (xp.reshape(bsz, seq, d), xs.reshape(db, 1, d), st("kp"), st("vp"), st("fp"),
            st("ks"), st("vs"), st("fs"), st("gs"))
```

```python
import functools

import jax
import jax.numpy as jnp
import numpy as np
from jax import lax
from jax.experimental import pallas as pl
from jax.experimental.pallas import tpu as pltpu

EPS = 1e-6
NEG_INF = -1e30
LOG2E = 1.4426950408889634
LANES = 128
BF16_ROWS = 16
F32 = jnp.float32
BF16 = jnp.bfloat16
VMEM_LIMIT = 56 * 1024 * 1024


def _dot(a, b):
    return jnp.dot(a, b, preferred_element_type=F32)


def _rms(x, g):
    ms = jnp.mean(x * x, axis=-1, keepdims=True)
    return x * lax.rsqrt(ms + EPS) * g


def _group_rms(x, g, bd):
    ms = _dot((x * x).astype(BF16), bd)
    return x * lax.rsqrt(ms + EPS) * g


def _split3(x):
    hi = x.astype(BF16)
    r = x - hi.astype(F32)
    mid = r.astype(BF16)
    lo = (r - mid.astype(F32)).astype(BF16)
    return hi, mid, lo


def _const_spec(shape):
    nd = len(shape)
    return pl.BlockSpec(shape, lambda *_: (0,) * nd, pipeline_mode=pl.Buffered(1))


def _params(n_axes):
    return pltpu.CompilerParams(dimension_semantics=("arbitrary",) * n_axes,
                                vmem_limit_bytes=VMEM_LIMIT)


def _ffn_kernel(x_ref, g_ref, wgu_ref, wd_ref, o_ref, act_ref, *, d_ff, chunk):
    x = x_ref[...]
    h = _rms(x, g_ref[...]).astype(BF16)
    for c in range(d_ff // chunk):
        a = _dot(h, wgu_ref[:, c * chunk:(c + 1) * chunk])
        b = _dot(h, wgu_ref[:, d_ff + c * chunk:d_ff + (c + 1) * chunk])
        act_ref[:, c * chunk:(c + 1) * chunk] = (jax.nn.silu(a) * b).astype(BF16)
    o_ref[...] = x + 0.5 * _dot(act_ref[...], wd_ref[...])


def _ffn_call(x, g, wgu, wd, tm):
    n, d = x.shape
    d_ff = wd.shape[0]
    chunk = 256 if d_ff % 256 == 0 else LANES
    return pl.pallas_call(
        functools.partial(_ffn_kernel, d_ff=d_ff, chunk=chunk),
        out_shape=jax.ShapeDtypeStruct((n, d), F32),
        grid=(n // tm,),
        in_specs=[pl.BlockSpec((tm, d), lambda i: (i, 0)),
                  _const_spec(g.shape), _const_spec(wgu.shape), _const_spec(wd.shape)],
        out_specs=pl.BlockSpec((tm, d), lambda i: (i, 0)),
        scratch_shapes=[pltpu.VMEM((tm, d_ff), BF16)],
        compiler_params=_params(1),
        name="ffn",
    )(x, g, wgu, wd)


def _mixer_common(x_ref, g_ref, wqkv_ref, wf_ref, bf_ref, wugv_ref, qg_ref, kg_ref, gvg_ref, bd_ref):
    h = _rms(x_ref[...], g_ref[...]).astype(BF16)
    aw = qg_ref.shape[1]
    qkv = _dot(h, wqkv_ref[...])
    q = _group_rms(qkv[:, :aw], qg_ref[...], bd_ref[...])
    k = _group_rms(qkv[:, aw:2 * aw], kg_ref[...], bd_ref[...])
    v = qkv[:, 2 * aw:]
    logf = jax.nn.log_sigmoid(_dot(h, wf_ref[...]) + bf_ref[...])
    ugv = _dot(h, wugv_ref[...])
    gw = gvg_ref.shape[1]
    u = jax.nn.gelu(ugv[:, :gw])
    gv = _group_rms(jax.nn.gelu(ugv[:, gw:]), gvg_ref[...], bd_ref[...])
    return q, k, v, logf, u, gv


def _mixer_prompt_kernel(x_ref, g_ref, wqkv_ref, wf_ref, bf_ref, wugv_ref, qg_ref, kg_ref, gvg_ref,
                         bd_ref, tri_ref, selqt_ref, selk_ref, ws_ref, bs_ref,
                         qt_ref, qat_ref, k_ref, kp_ref, ka_ref, v_ref, vt_ref, lf_ref, mg_ref,
                         carry_ref, *, tiles_per_seq, n_heads, chunk):
    q, k, v, logf, u, gv = _mixer_common(x_ref, g_ref, wqkv_ref, wf_ref, bf_ref, wugv_ref,
                                         qg_ref, kg_ref, gvg_ref, bd_ref)
    tm = q.shape[0]
    qt_ref[0] = q.T.astype(BF16)
    k_ref[0] = k.T
    kp_ref[...] = k.astype(BF16)
    v_t = v.T
    v_ref[0] = v_t
    vt_ref[0] = v_t.astype(BF16)
    lf_ref[0] = logf.T[:n_heads]

    @pl.when(pl.program_id(0) % tiles_per_seq == 0)
    def _():
        carry_ref[...] = jnp.zeros_like(carry_ref)

    lane = lax.broadcasted_iota(jnp.int32, logf.shape, 1)
    lf = jnp.where(lane < n_heads, logf, 0.0)
    pieces = jnp.concatenate(_split3(lf), axis=-1)
    cs = _dot(tri_ref[...], pieces)
    f_cum = cs[:, :LANES] + cs[:, LANES:2 * LANES] + cs[:, 2 * LANES:] + carry_ref[...]
    carry_ref[...] = f_cum[tm - 1:tm, :]
    f_ext = jnp.where(lane == n_heads, 1.0, f_cum * LOG2E)
    ka_ref[...] = _dot(jnp.concatenate(_split3(f_ext), axis=-1), selk_ref[...]).astype(BF16)
    fpt = jnp.concatenate(_split3(f_ext.T), axis=0)
    qat_ref[0] = _dot(selqt_ref[...], fpt).astype(BF16)

    gvb = gv.astype(BF16)
    r_i = lax.broadcasted_iota(jnp.int32, (chunk, chunk), 0)
    c_i = lax.broadcasted_iota(jnp.int32, (chunk, chunk), 1)
    lane_c = lax.broadcasted_iota(jnp.int32, (chunk, LANES), 1)
    n_groups = ws_ref.shape[0]
    gdim = gvb.shape[1] // n_groups
    per_blk = LANES // gdim
    ws = [jnp.where(c_i <= r_i, ws_ref[g], jnp.zeros((), BF16)) for g in range(n_groups)]
    for c in range(tm // chunk):
        rows = slice(c * chunk, (c + 1) * chunk)
        for jb in range(gvb.shape[1] // LANES):
            cols = slice(jb * LANES, (jb + 1) * LANES)
            blk = gvb[rows, cols]
            sp = _dot(ws[jb * per_blk], blk)
            for gi in range(1, per_blk):
                sp = jnp.where(lane_c >= gi * gdim, _dot(ws[jb * per_blk + gi], blk), sp)
            mg_ref[rows, cols] = (u[rows, cols] * (sp + bs_ref[:, cols])).astype(BF16)


def _mixer_sample_kernel(x_ref, g_ref, wqkv_ref, wf_ref, bf_ref, wugv_ref, qg_ref, kg_ref, gvg_ref,
                         bd_ref, w00_ref, b0_ref,
                         q_ref, k_ref, v_ref, lf_ref, gv_ref, mg_ref):
    q, k, v, logf, u, gv = _mixer_common(x_ref, g_ref, wqkv_ref, wf_ref, bf_ref, wugv_ref,
                                         qg_ref, kg_ref, gvg_ref, bd_ref)
    q_ref[...] = q
    k_ref[...] = k
    v_ref[...] = v
    lf_ref[...] = logf
    gv_ref[...] = gv
    mg_ref[...] = (u * (gv * w00_ref[...] + b0_ref[...])).astype(BF16)


def _attn_kernel(qt_ref, qat_ref, kp_ref, ka_ref, vt_ref, o_ref, m_ref, acc_ref,
                 s_ref, p_ref, al_ref, *, tq, hd):
    i = pl.program_id(2)
    n_sub = LANES // hd
    qcat = jnp.concatenate([qt_ref[0], qat_ref[0]], axis=0)
    row_q = lax.broadcasted_iota(jnp.int32, qcat.shape, 0) & (LANES - 1)
    qs = [jnp.where((row_q >= h * hd) & (row_q < (h + 1) * hd), qcat, jnp.zeros((), BF16))
          for h in range(n_sub)]
    m_ref[...] = jnp.full_like(m_ref, NEG_INF)
    acc_ref[...] = jnp.zeros_like(acc_ref)
    p_ref[1] = jnp.zeros_like(p_ref[1])
    al_ref[1] = jnp.ones_like(al_ref[1])

    def scores(t, slot):
        start = pl.multiple_of(t * tq, tq)
        kc = jnp.concatenate([kp_ref[0, pl.ds(start, tq), :], ka_ref[0, pl.ds(start, tq), :]], axis=-1)
        for h in range(n_sub):
            s_ref[slot, h] = _dot(kc, qs[h])

    def softmax(slot, masked):
        for h in range(n_sub):
            s = s_ref[slot, h]
            if masked:
                key = lax.broadcasted_iota(jnp.int32, s.shape, 0)
                qry = lax.broadcasted_iota(jnp.int32, s.shape, 1)
                s = jnp.where(key <= qry, s, NEG_INF)
            m_prev = m_ref[h]
            m_next = jnp.maximum(m_prev, jnp.max(s, axis=0, keepdims=True))
            alpha = jnp.exp2(m_prev - m_next)
            p = jnp.exp2(s - m_next)
            m_ref[h] = m_next
            al_ref[slot, h] = alpha
            p_ref[slot, h] = p.astype(BF16)

    def values(t, slot):
        start = pl.multiple_of(jnp.maximum(t, 0) * tq, tq)
        for h in range(n_sub):
            v_t = jnp.concatenate([vt_ref[0, h * hd:(h + 1) * hd, pl.ds(start, tq)],
                                   jnp.ones((BF16_ROWS, tq), BF16)], axis=0)
            acc_ref[h] = al_ref[slot, h] * acc_ref[h] + _dot(v_t, p_ref[slot, h])

    def stage(t, slot):
        scores(t + 1, 1 - slot)
        softmax(slot, False)
        values(t - 1, 1 - slot)

    def body(u, c):
        stage(2 * u, 0)
        stage(2 * u + 1, 1)
        return c

    scores(0, 0)
    lax.fori_loop(0, i // 2, body, 0)

    @pl.when(i % 2 == 1)
    def _():
        stage(i - 1, 0)
        softmax(1, True)
        values(i - 1, 0)
        values(i, 1)

    @pl.when(i % 2 == 0)
    def _():
        softmax(0, True)
        values(i - 1, 1)
        values(i, 0)

    out_t = jnp.concatenate([acc_ref[h, :hd] / acc_ref[h, hd:hd + 1] for h in range(n_sub)], axis=0)
    o_ref[0] = out_t.T.astype(o_ref.dtype)


def _attn_call(qt, qat, kp, ka, vt, tq, hd):
    b, s, w = kp.shape
    n_sub = LANES // hd
    qt_spec = pl.BlockSpec((1, LANES, tq), lambda bi, hp, i: (bi, hp, i))
    k_spec = pl.BlockSpec((1, s, LANES), lambda bi, hp, i: (bi, 0, hp))
    return pl.pallas_call(
        functools.partial(_attn_kernel, tq=tq, hd=hd),
        out_shape=jax.ShapeDtypeStruct((b, s, w), BF16),
        grid=(b, w // LANES, s // tq),
        in_specs=[qt_spec, qt_spec, k_spec, k_spec,
                  pl.BlockSpec((1, LANES, s), lambda bi, hp, i: (bi, hp, 0))],
        out_specs=pl.BlockSpec((1, tq, LANES), lambda bi, hp, i: (bi, i, hp)),
        scratch_shapes=[pltpu.VMEM((n_sub, 1, tq), F32),
                        pltpu.VMEM((n_sub, hd + BF16_ROWS, tq), F32), pltpu.VMEM((2, n_sub, tq, tq), F32),
                        pltpu.VMEM((2, n_sub, tq, tq), BF16), pltpu.VMEM((2, n_sub, 1, tq), F32)],
        compiler_params=_params(3),
        name="fox_attn_prompt",
    )(qt, qat, kp, ka, vt)


def _decode_init(fn_ref, m_ref, l_ref, acc_ref, car_ref):
    m_ref[...] = jnp.full_like(m_ref, NEG_INF)
    l_ref[...] = jnp.zeros_like(l_ref)
    acc_ref[...] = jnp.zeros_like(acc_ref)
    car_ref[...] = jnp.broadcast_to(fn_ref[0], car_ref.shape)


def _decode_chunk(k_pages, v_pages, lf_pages, qcol_ref, ltri_ref, m_ref, l_ref, acc_ref, car_ref,
                  *, n_heads, hd):
    g_pages = len(k_pages)
    page = ltri_ref.shape[0]
    lf = jnp.concatenate([r[...] for r in lf_pages], axis=0)
    both = _dot(jnp.concatenate(_split3(lf), axis=0), ltri_ref[...])
    gh = g_pages * n_heads
    both = both[:gh] + both[gh:2 * gh] + both[2 * gh:]
    inner = both[:, :page]
    total = both[:, page:]
    carry = car_ref[...]
    bias = [None] * g_pages
    for r in reversed(range(g_pages)):
        bias[r] = inner[r * n_heads:(r + 1) * n_heads] + carry
        carry = carry + total[r * n_heads:(r + 1) * n_heads]
    car_ref[...] = carry

    s_rows = [[None] * n_heads for _ in range(g_pages)]
    for h in range(n_heads):
        hs = slice(h * hd, (h + 1) * hd)
        qh = qcol_ref[0, hs, :]
        for r in range(g_pages):
            s_rows[r][h] = jnp.sum(k_pages[r][hs, :] * qh, axis=0, keepdims=True)
    s = [jnp.concatenate(s_rows[r], axis=0) + bias[r] * LOG2E for r in range(g_pages)]
    m_prev = m_ref[...]
    m_next = m_prev
    for r in range(g_pages):
        m_next = jnp.maximum(m_next, s[r])
    m_next = jnp.broadcast_to(jnp.max(m_next, axis=-1, keepdims=True), m_prev.shape)
    alpha = jnp.exp2(m_prev - m_next)
    p = [jnp.exp2(s[r] - m_next) for r in range(g_pages)]
    l_ref[...] = alpha * l_ref[...] + sum(p[1:], p[0])
    for h in range(n_heads):
        hs = slice(h * hd, (h + 1) * hd)
        pv = v_pages[0][hs, :] * p[0][h:h + 1, :]
        for r in range(1, g_pages):
            pv = pv + v_pages[r][hs, :] * p[r][h:h + 1, :]
        acc_ref[hs, :] = alpha[h:h + 1, :] * acc_ref[hs, :] + pv
    m_ref[...] = m_next


def _decode_finish(q_ref, kn_ref, vn_ref, m_ref, l_ref, acc_ref, o_ref, *, n_heads, hd):
    width = n_heads * hd
    row = lax.broadcasted_iota(jnp.int32, (n_heads, width), 0)
    col = lax.broadcasted_iota(jnp.int32, (n_heads, width), 1)
    head_mask = (col >= row * hd) & (col < (row + 1) * hd)

    def head_row(x):
        return jnp.sum(jnp.where(head_mask, jnp.broadcast_to(x, (n_heads, width)), 0.0),
                       axis=0, keepdims=True)

    qbd = jnp.where(head_mask, jnp.broadcast_to(q_ref[0], (n_heads, width)), 0.0)
    s_new = jnp.sum(qbd * kn_ref[0], axis=-1, keepdims=True)
    m_old = m_ref[:, :1]
    m_fin = jnp.maximum(m_old, s_new)
    a_fin = jnp.exp2(m_old - m_fin)
    p_new = jnp.exp2(s_new - m_fin)
    l_fin = a_fin * jnp.sum(l_ref[...], axis=-1, keepdims=True) + p_new
    past = jnp.sum(acc_ref[...].T, axis=0, keepdims=True)
    out = (head_row(a_fin) * past + head_row(p_new) * vn_ref[0]) / head_row(l_fin)
    o_ref[0] = out.astype(o_ref.dtype)


def _ffn_decode_kernel(pt_ref, x_ref, g_ref, wgu_ref, wd_ref, qcol_ref, q_ref, kn_ref, vn_ref, fn_ref,
                       ltri_ref, ck_hbm, cv_hbm, clf_hbm, o_ref, a_ref,
                       act_ref, kbuf, vbuf, lfbuf, sem, m_ref, l_ref, acc_ref, car_ref,
                       *, d_ff, chunk, down_block, b0, g_pages, n_chunks, steps_per_sample, n_heads, hd):
    step = pl.program_id(0)
    n_steps = pl.num_programs(0)
    n_groups = steps_per_sample * n_chunks

    def copies(st, c, slot):
        b = b0 + st // steps_per_sample
        first = (n_groups - 1 - ((st % steps_per_sample) * n_chunks + c)) * g_pages
        out = []
        for r in range(g_pages):
            pid = pt_ref[b, first + r]
            out.append(pltpu.make_async_copy(ck_hbm.at[pid], kbuf.at[slot, r], sem.at[slot, 0]))
            out.append(pltpu.make_async_copy(cv_hbm.at[pid], vbuf.at[slot, r], sem.at[slot, 1]))
            out.append(pltpu.make_async_copy(clf_hbm.at[pid], lfbuf.at[slot, r], sem.at[slot, 2]))
        return out

    @pl.when(step == 0)
    def _():
        for cp in copies(step, 0, 0):
            cp.start()

    @pl.when(step % steps_per_sample == 0)
    def _():
        _decode_init(fn_ref, m_ref, l_ref, acc_ref, car_ref)

    def decode(c):
        slot = c % 2
        if c + 1 < n_chunks:
            for cp in copies(step, c + 1, 1 - slot):
                cp.start()
        else:
            @pl.when(step + 1 < n_steps)
            def _():
                for cp in copies(step + 1, 0, 1 - slot):
                    cp.start()
        for cp in copies(step, c, slot):
            cp.wait()
        _decode_chunk([kbuf.at[slot, r] for r in range(g_pages)],
                      [vbuf.at[slot, r] for r in range(g_pages)],
                      [lfbuf.at[slot, r] for r in range(g_pages)],
                      qcol_ref, ltri_ref, m_ref, l_ref, acc_ref, car_ref, n_heads=n_heads, hd=hd)

    x = x_ref[...]
    h = _rms(x, g_ref[...]).astype(BF16)

    def up(c):
        a = _dot(h, wgu_ref[:, c * chunk:(c + 1) * chunk])
        b = _dot(h, wgu_ref[:, d_ff + c * chunk:d_ff + (c + 1) * chunk])
        act_ref[:, c * chunk:(c + 1) * chunk] = (jax.nn.silu(a) * b).astype(BF16)

    def down(nb):
        cols = slice(nb * down_block, (nb + 1) * down_block)
        o_ref[:, cols] = x[:, cols] + 0.5 * _dot(act_ref[...], wd_ref[:, cols])

    units = ([functools.partial(up, c) for c in range(d_ff // chunk)]
             + [functools.partial(down, nb) for nb in range(x.shape[1] // down_block)])
    chunk_at = {(c * len(units)) // n_chunks: c for c in range(n_chunks)}
    for u, unit in enumerate(units):
        if u in chunk_at:
            decode(chunk_at[u])
        unit()

    @pl.when(step % steps_per_sample == steps_per_sample - 1)
    def _():
        _decode_finish(q_ref, kn_ref, vn_ref, m_ref, l_ref, acc_ref, a_ref, n_heads=n_heads, hd=hd)


def _ffn_decode_call(x, g, wgu, wd, tm, page_table, b0, n_samples, qcol, q, kn, vn, fn,
                     cache_kt, cache_vt, cache_lf_t, g_pages, n_heads, hd):
    n, d = x.shape
    d_ff = wd.shape[0]
    chunk = 256 if d_ff % 256 == 0 else LANES
    n_steps = n // tm
    n_pages = page_table.shape[1]
    n_pool, width, page = cache_kt.shape
    assert n_steps % n_samples == 0
    steps_per_sample = n_steps // n_samples
    assert n_pages % (steps_per_sample * g_pages) == 0
    n_chunks = n_pages // (steps_per_sample * g_pages)
    down_block = 512 if d % 512 == 0 else d
    assert n_chunks % 2 == 0 and n_chunks <= d_ff // chunk + d // down_block, "ring slots alternate per chunk"
    ltri = jnp.asarray(np.concatenate([np.tril(np.ones((page, page), np.float32), -1),
                                       np.ones((page, page), np.float32)], axis=1), BF16)
    sample = lambda i, pt: (b0 + i // steps_per_sample, 0, 0)
    const = lambda shape: pl.BlockSpec(shape, lambda i, pt: (0,) * len(shape), pipeline_mode=pl.Buffered(1))
    row_spec = pl.BlockSpec((1, 1, width), sample)
    hbm = pl.BlockSpec(memory_space=pl.ANY)
    return pl.pallas_call(
        functools.partial(_ffn_decode_kernel, d_ff=d_ff, chunk=chunk, down_block=down_block, b0=b0, g_pages=g_pages,
                          n_chunks=n_chunks, steps_per_sample=steps_per_sample, n_heads=n_heads, hd=hd),
        out_shape=(jax.ShapeDtypeStruct((n, d), F32), jax.ShapeDtypeStruct((n_samples, 1, width), BF16)),
        grid_spec=pltpu.PrefetchScalarGridSpec(
            num_scalar_prefetch=1,
            grid=(n_steps,),
            in_specs=[pl.BlockSpec((tm, d), lambda i, pt: (i, 0)),
                      const(g.shape), const(wgu.shape), const(wd.shape),
                      pl.BlockSpec((1, width, page), sample), row_spec, row_spec, row_spec,
                      pl.BlockSpec((1, n_heads, page), sample), const(ltri.shape), hbm, hbm, hbm],
            out_specs=(pl.BlockSpec((tm, d), lambda i, pt: (i, 0)),
                       pl.BlockSpec((1, 1, width), lambda i, pt: (i // steps_per_sample, 0, 0))),
            scratch_shapes=[pltpu.VMEM((tm, d_ff), BF16),
                            pltpu.VMEM((2, g_pages, width, page), F32),
                            pltpu.VMEM((2, g_pages, width, page), F32),
                            pltpu.VMEM((2, g_pages, n_heads, page), F32),
                            pltpu.SemaphoreType.DMA((2, 3)),
                            pltpu.VMEM((n_heads, page), F32), pltpu.VMEM((n_heads, page), F32),
                            pltpu.VMEM((width, page), F32), pltpu.VMEM((n_heads, page), F32)]),
        compiler_params=_params(1),
        name="ffn_decode",
    )(page_table, x, g, wgu, wd, qcol, q, kn, vn, fn, ltri, cache_kt, cache_vt, cache_lf_t)


def _merge_kernel(x_ref, a_ref, mg_ref, g_ref, wga_ref, wgb_ref, wpa_ref, wpg_ref, wo_ref, o_ref):
    x = x_ref[...]
    h = _rms(x, g_ref[...]).astype(BF16)
    ga = jax.nn.sigmoid(_dot(h, wga_ref[...]))
    gb = jax.nn.sigmoid(_dot(h, wgb_ref[...]))
    merged = ga * _dot(a_ref[...], wpa_ref[...]) + gb * _dot(mg_ref[...], wpg_ref[...])
    o_ref[...] = x + _dot(merged.astype(BF16), wo_ref[...])


def _merge_call(x, a, mg, g, wga, wgb, wpa, wpg, wo, tm):
    n, d = x.shape
    w = a.shape[1]
    row = lambda cols: pl.BlockSpec((tm, cols), lambda i: (i, 0))
    return pl.pallas_call(
        _merge_kernel,
        out_shape=jax.ShapeDtypeStruct((n, d), F32),
        grid=(n // tm,),
        in_specs=[row(d), row(w), row(mg.shape[1])] + [_const_spec(t.shape) for t in (g, wga, wgb, wpa, wpg, wo)],
        out_specs=row(d),
        compiler_params=_params(1),
        name="merge_out",
    )(x, a, mg, g, wga, wgb, wpa, wpg, wo)


def _ple_kernel(x_ref, p_ref, g_ref, wg_ref, wp_ref, o_ref):
    x = x_ref[...]
    gate = jax.nn.sigmoid(_dot(_rms(x, g_ref[...]).astype(BF16), wg_ref[...]))
    o_ref[...] = x + gate * _dot(p_ref[...].astype(BF16), wp_ref[...])


def _ple_call(x, p, g, wg, wp, tm):
    n, d = x.shape
    row = lambda cols: pl.BlockSpec((tm, cols), lambda i: (i, 0))
    return pl.pallas_call(
        _ple_kernel,
        out_shape=jax.ShapeDtypeStruct((n, d), F32),
        grid=(n // tm,),
        in_specs=[row(d), row(p.shape[1])] + [_const_spec(t.shape) for t in (g, wg, wp)],
        out_specs=row(d),
        compiler_params=_params(1),
        name="ple",
    )(x, p, g, wg, wp)


def _block_diag_mean(width, group):
    idx = np.arange(width) // group
    return jnp.asarray((idx[:, None] == idx[None, :]).astype(np.float32) / group, BF16)


def _forget_selectors(n_heads, hd):
    width = n_heads * hd
    selq = np.zeros((3 * LANES, width), np.float32)
    selk = np.zeros((3 * LANES, width), np.float32)
    for h in range(n_heads):
        for p in range(3):
            selq[p * LANES + h, h * hd + p] = 1.0
            selk[p * LANES + h, h * hd + 3 + p] = -1.0
            selq[n_heads, h * hd + 3 + p] = 1.0
            selk[n_heads, h * hd + p] = 1.0
    return jnp.asarray(selq.T, BF16), jnp.asarray(selk, BF16)


def kernel(x_prompt, x_sample, cache_k, cache_v, cache_logf, page_table, p_prompt, p_sample,
           ffn1_norm, ffn1_w_gu, ffn1_w_down, mix_norm, w_in, b_forget, q_norm, k_norm,
           gmlp_v_norm, w_spatial, b_spatial, w_proj_attn, w_proj_gmlp, w_out,
           ffn2_norm, ffn2_w_gu, ffn2_w_down, ple_norm, ple_w_gate, ple_w_proj):
    depth = ffn1_norm.shape[0]
    bsz, seq, d = x_prompt.shape
    db = x_sample.shape[0]
    assert x_sample.shape[1] == 1, "the sample group decodes one token per step"
    n_heads, hd = cache_k.shape[3], cache_k.shape[4]
    aw = n_heads * hd
    n_groups, chunk = w_spatial.shape[1], w_spatial.shape[2]
    gw = w_proj_gmlp.shape[1]
    gdim = gw // n_groups
    page = cache_k.shape[2]
    n_pool = cache_k.shape[1]
    assert hd == gdim and aw == gw and LANES % hd == 0 and n_heads <= LANES and page == LANES

    tm = 512
    tq = 512
    g_pages = 8
    scale = hd ** -0.5
    tri = jnp.asarray(np.tril(np.ones((tm, tm), np.float32)), BF16)
    bd = _block_diag_mean(aw, hd)
    selqt, selk = _forget_selectors(n_heads, hd)

    xp = x_prompt.reshape(bsz * seq, d)
    xs = x_sample.reshape(db, d)
    outs = {k: [] for k in ("kp", "vp", "fp", "ks", "vs", "fs", "gs")}
    for li in range(depth):
        bf = lambda t: t[li].astype(BF16)
        row = lambda t: t[li].reshape(1, -1).astype(F32)
        tile_h = lambda t, mult=1.0: jnp.tile(t[li].astype(F32) * mult, n_heads).reshape(1, -1)
        w1gu, w1d, w2gu, w2d = bf(ffn1_w_gu), bf(ffn1_w_down), bf(ffn2_w_gu), bf(ffn2_w_down)
        wi = w_in[li]
        o = 0
        wqkv = wi[:, o:o + 3 * aw].astype(BF16); o += 3 * aw
        wf = jnp.pad(wi[:, o:o + n_heads], ((0, 0), (0, LANES - n_heads))).astype(BF16); o += n_heads
        wugv = wi[:, o:o + 2 * gw].astype(BF16); o += 2 * gw
        wga = wi[:, o:o + d].astype(BF16); o += d
        wgb = wi[:, o:o + d].astype(BF16)
        bfg = jnp.pad(b_forget[li].astype(F32), (0, LANES - n_heads)).reshape(1, LANES)
        qg, kg, gvg = tile_h(q_norm, scale * LOG2E), tile_h(k_norm), tile_h(gmlp_v_norm)
        ws = bf(w_spatial)
        bs = jnp.repeat(b_spatial[li].astype(F32).T, gdim, axis=1)
        w00 = jnp.repeat(w_spatial[li, :, 0, 0].astype(F32), gdim).reshape(1, gw)
        b0 = jnp.repeat(b_spatial[li, :, 0].astype(F32), gdim).reshape(1, gw)
        wpa, wpg, wo = bf(w_proj_attn), bf(w_proj_gmlp), bf(w_out)
        plg, plp = bf(ple_w_gate), bf(ple_w_proj)
        mixer_w = (row(mix_norm), wqkv, wf, bfg, wugv, qg, kg, gvg, bd)

        x1s = _ffn_call(xs, row(ffn1_norm), w1gu, w1d, db)
        sconsts = mixer_w + (w00, b0)
        full = lambda cols: pl.BlockSpec((db, cols), lambda i: (0, 0))
        ssd = lambda cols, dt: jax.ShapeDtypeStruct((db, cols), dt)
        qs, ksn, vsn, lfs, gvs, mgs = pl.pallas_call(
            _mixer_sample_kernel,
            out_shape=(ssd(aw, F32), ssd(aw, F32), ssd(aw, F32), ssd(LANES, F32), ssd(gw, F32), ssd(gw, BF16)),
            grid=(1,),
            in_specs=[full(d)] + [_const_spec(t.shape) for t in sconsts],
            out_specs=tuple(full(c) for c in (aw, aw, aw, LANES, gw, gw)),
            compiler_params=_params(1),
            name="mixer_sample",
        )(x1s, *sconsts)
        fn = jnp.broadcast_to(lfs[:, :n_heads, None], (db, n_heads, page))
        qcol = jnp.broadcast_to(qs[:, :, None], (db, aw, page))
        cache_lf_t = jnp.swapaxes(cache_logf[li].astype(F32), 1, 2)
        cache_kt = jnp.transpose(cache_k[li], (0, 2, 3, 1)).reshape(n_pool, aw, page)
        cache_vt = jnp.transpose(cache_v[li], (0, 2, 3, 1)).reshape(n_pool, aw, page)
        decode_args = (qcol, qs.reshape(db, 1, aw), ksn.reshape(db, 1, aw), vsn.reshape(db, 1, aw), fn,
                       cache_kt, cache_vt, cache_lf_t, g_pages, n_heads, hd)
        half = db // 2

        n = bsz * seq
        x1, a_s0 = _ffn_decode_call(xp, row(ffn1_norm), w1gu, w1d, tm, page_table, 0, half, *decode_args)
        rowspec = lambda cols: pl.BlockSpec((tm, cols), lambda i: (i, 0))
        consts = mixer_w + (tri, selqt, selk, ws, bs)
        sds = lambda cols, dt: jax.ShapeDtypeStruct((n, cols), dt)
        tps = seq // tm
        tsd = lambda rows_, dt: jax.ShapeDtypeStruct((bsz, rows_, seq), dt)
        tspec = lambda rows_: pl.BlockSpec((1, rows_, tm), lambda i: (i // tps, 0, i % tps))
        qt, qat, k32t, kp, ka, v32t, vt, lfpt, mg = pl.pallas_call(
            functools.partial(_mixer_prompt_kernel, tiles_per_seq=tps, n_heads=n_heads, chunk=chunk),
            out_shape=(tsd(aw, BF16), tsd(aw, BF16), tsd(aw, F32), sds(aw, BF16), sds(aw, BF16),
                       tsd(aw, F32), tsd(aw, BF16), tsd(n_heads, F32), sds(gw, BF16)),
            grid=(n // tm,),
            in_specs=[rowspec(d)] + [_const_spec(t.shape) for t in consts],
            out_specs=(tspec(aw), tspec(aw), tspec(aw), rowspec(aw), rowspec(aw),
                       tspec(aw), tspec(aw), tspec(n_heads), rowspec(gw)),
            scratch_shapes=[pltpu.VMEM((1, LANES), F32)],
            compiler_params=_params(1),
            name="mixer_prompt",
        )(x1, *consts)
        r3 = lambda t: t.reshape(bsz, seq, aw)
        a = _attn_call(qt, qat, r3(kp), r3(ka), vt, tq, hd).reshape(n, aw)
        x2 = _merge_call(x1, a, mg, row(mix_norm), wga, wgb, wpa, wpg, wo, tm)
        x3, a_s1 = _ffn_decode_call(x2, row(ffn2_norm), w2gu, w2d, tm, page_table, half, db - half,
                                    *decode_args)
        xp = _ple_call(x3, p_prompt[li].reshape(n, -1), row(ple_norm), plg, plp, tm)
        outs["kp"].append(k32t.reshape(bsz, n_heads, hd, seq).transpose(0, 3, 1, 2))
        outs["vp"].append(v32t.reshape(bsz, n_heads, hd, seq).transpose(0, 3, 1, 2))
        outs["fp"].append(lfpt.transpose(0, 2, 1))

        a_s = jnp.concatenate([a_s0, a_s1], axis=0).reshape(db, aw)
        x2s = _merge_call(x1s, a_s, mgs, row(mix_norm), wga, wgb, wpa, wpg, wo, db)
        x3s = _ffn_call(x2s, row(ffn2_norm), w2gu, w2d, db)
        xs = _ple_call(x3s, p_sample[li].reshape(db, -1), row(ple_norm), plg, plp, db)
        outs["ks"].append(ksn.reshape(db, 1, n_heads, hd))
        outs["vs"].append(vsn.reshape(db, 1, n_heads, hd))
        outs["fs"].append(lfs[:, :n_heads].reshape(db, 1, n_heads))
        outs["gs"].append(gvs.reshape(db, 1, n_groups, gdim))

    st = lambda key: jnp.stack(outs[key])
    return (xp.reshape(bsz, seq, d), xs.reshape(db, 1, d), st("kp"), st("vp"), st("fp"),
            st("ks"), st("vs"), st("fs"), st("gs"))
```

```python
import functools
import math

import jax
import jax.numpy as jnp
import numpy as np
from jax import lax
from jax.experimental import pallas as pl
from jax.experimental.pallas import tpu as pltpu

EPS = 1e-6
NEG_INF = -1e30
LOG2E = 1.4426950408889634
LANES = 128
BF16_ROWS = 16
MXU_TILE = 256
F32 = jnp.float32
BF16 = jnp.bfloat16
VMEM_LIMIT = 56 * 1024 * 1024


def _dot(a, b):
    return jnp.dot(a, b, preferred_element_type=F32)


def _rms(x, g):
    ms = jnp.mean(x * x, axis=-1, keepdims=True)
    return x * lax.rsqrt(ms + EPS) * g


def _group_rms(x, g, bd):
    x2 = (x * x).astype(BF16)
    w = bd.shape[0]
    ms = jnp.concatenate([_dot(x2[:, i * w:(i + 1) * w], bd) for i in range(x.shape[1] // w)], axis=-1)
    return x * lax.rsqrt(ms + EPS) * g


def _split3(x):
    hi = x.astype(BF16)
    r = x - hi.astype(F32)
    mid = r.astype(BF16)
    lo = (r - mid.astype(F32)).astype(BF16)
    return hi, mid, lo


def _const_spec(shape):
    nd = len(shape)
    return pl.BlockSpec(shape, lambda *_: (0,) * nd, pipeline_mode=pl.Buffered(1))


def _params(n_axes):
    return pltpu.CompilerParams(dimension_semantics=("arbitrary",) * n_axes,
                                vmem_limit_bytes=VMEM_LIMIT)


def _ffn_kernel(x_ref, g_ref, wgu_ref, wd_ref, o_ref, act_ref, *, d_ff, chunk):
    x = x_ref[...]
    h = _rms(x, g_ref[...]).astype(BF16)
    for c in range(d_ff // chunk):
        a = _dot(h, wgu_ref[:, c * chunk:(c + 1) * chunk])
        b = _dot(h, wgu_ref[:, d_ff + c * chunk:d_ff + (c + 1) * chunk])
        act_ref[:, c * chunk:(c + 1) * chunk] = (jax.nn.silu(a) * b).astype(BF16)
    o_ref[...] = x + 0.5 * _dot(act_ref[...], wd_ref[...])


def _ffn_call(x, g, wgu, wd, tm):
    n, d = x.shape
    d_ff = wd.shape[0]
    chunk = 256 if d_ff % 256 == 0 else LANES
    return pl.pallas_call(
        functools.partial(_ffn_kernel, d_ff=d_ff, chunk=chunk),
        out_shape=jax.ShapeDtypeStruct((n, d), F32),
        grid=(n // tm,),
        in_specs=[pl.BlockSpec((tm, d), lambda i: (i, 0)),
                  _const_spec(g.shape), _const_spec(wgu.shape), _const_spec(wd.shape)],
        out_specs=pl.BlockSpec((tm, d), lambda i: (i, 0)),
        scratch_shapes=[pltpu.VMEM((tm, d_ff), BF16)],
        compiler_params=_params(1),
        name="ffn",
    )(x, g, wgu, wd)


def _mixer_common(x_ref, g_ref, wqkv_ref, wf_ref, bf_ref, wugv_ref, qg_ref, kg_ref, gvg_ref, bd_ref):
    h = _rms(x_ref[...], g_ref[...]).astype(BF16)
    aw = qg_ref.shape[1]
    qkv = _dot(h, wqkv_ref[...])
    q = _group_rms(qkv[:, :aw], qg_ref[...], bd_ref[...])
    k = _group_rms(qkv[:, aw:2 * aw], kg_ref[...], bd_ref[...])
    v = qkv[:, 2 * aw:]
    logf = jax.nn.log_sigmoid(_dot(h, wf_ref[...]) + bf_ref[...])
    ugv = _dot(h, wugv_ref[...])
    gw = gvg_ref.shape[1]
    u = jax.nn.gelu(ugv[:, :gw])
    gv = _group_rms(jax.nn.gelu(ugv[:, gw:]), gvg_ref[...], bd_ref[...])
    return q, k, v, logf, u, gv


def _mixer_prompt_kernel(x_ref, g_ref, wqkv_ref, wf_ref, bf_ref, wugv_ref, qg_ref, kg_ref, gvg_ref,
                         bd_ref, tri_ref, selqt_ref, selk_ref, ws_ref, bs_ref,
                         qt_ref, qat_ref, k_ref, kp_ref, ka_ref, v_ref, vt_ref, lf_ref, mg_ref,
                         carry_ref, *, tiles_per_seq, n_heads, chunk):
    q, k, v, logf, u, gv = _mixer_common(x_ref, g_ref, wqkv_ref, wf_ref, bf_ref, wugv_ref,
                                         qg_ref, kg_ref, gvg_ref, bd_ref)
    tm = q.shape[0]
    qt_ref[0] = q.T.astype(BF16)
    k_ref[0] = k.T
    kp_ref[...] = k.astype(BF16)
    v_t = v.T
    v_ref[0] = v_t
    vt_ref[0] = v_t.astype(BF16)
    lf_ref[0] = logf.T[:n_heads]

    @pl.when(pl.program_id(0) % tiles_per_seq == 0)
    def _():
        carry_ref[...] = jnp.zeros_like(carry_ref)

    lane = lax.broadcasted_iota(jnp.int32, logf.shape, 1)
    lf = jnp.where(lane < n_heads, logf, 0.0)
    pieces = jnp.concatenate(_split3(lf), axis=-1)
    cs = _dot(tri_ref[...], pieces)
    f_cum = cs[:, :LANES] + cs[:, LANES:2 * LANES] + cs[:, 2 * LANES:] + carry_ref[...]
    carry_ref[...] = f_cum[tm - 1:tm, :]
    f_ext = jnp.where(lane == n_heads, 1.0, f_cum * LOG2E)
    ka_ref[...] = _dot(jnp.concatenate(_split3(f_ext), axis=-1), selk_ref[...]).astype(BF16)
    fpt = jnp.concatenate(_split3(f_ext.T), axis=0)
    qat_ref[0] = _dot(selqt_ref[...], fpt).astype(BF16)

    gvb = gv.astype(BF16)
    r_i = lax.broadcasted_iota(jnp.int32, (chunk, chunk), 0)
    c_i = lax.broadcasted_iota(jnp.int32, (chunk, chunk), 1)
    lane_c = lax.broadcasted_iota(jnp.int32, (chunk, LANES), 1)
    n_groups = ws_ref.shape[0]
    gdim = gvb.shape[1] // n_groups
    per_blk = LANES // gdim
    ws = [jnp.where(c_i <= r_i, ws_ref[g], jnp.zeros((), BF16)) for g in range(n_groups)]
    for c in range(tm // chunk):
        rows = slice(c * chunk, (c + 1) * chunk)
        for jb in range(gvb.shape[1] // LANES):
            cols = slice(jb * LANES, (jb + 1) * LANES)
            blk = gvb[rows, cols]
            sp = _dot(ws[jb * per_blk], blk)
            for gi in range(1, per_blk):
                sp = jnp.where(lane_c >= gi * gdim, _dot(ws[jb * per_blk + gi], blk), sp)
            mg_ref[rows, cols] = (u[rows, cols] * (sp + bs_ref[:, cols])).astype(BF16)


def _mixer_sample_kernel(x_ref, g_ref, wqkv_ref, wf_ref, bf_ref, wugv_ref, qg_ref, kg_ref, gvg_ref,
                         bd_ref, w00_ref, b0_ref,
                         q_ref, k_ref, v_ref, lf_ref, gv_ref, mg_ref):
    q, k, v, logf, u, gv = _mixer_common(x_ref, g_ref, wqkv_ref, wf_ref, bf_ref, wugv_ref,
                                         qg_ref, kg_ref, gvg_ref, bd_ref)
    q_ref[...] = q
    k_ref[...] = k
    v_ref[...] = v
    lf_ref[...] = logf
    gv_ref[...] = gv
    mg_ref[...] = (u * (gv * w00_ref[...] + b0_ref[...])).astype(BF16)


def _attn_kernel(qt_ref, qat_ref, kp_ref, ka_ref, vt_ref, o_ref, m_ref, acc_ref,
                 s_ref, p_ref, al_ref, *, tq, hd):
    i = pl.program_id(2)
    n_sub = LANES // hd
    qcat = jnp.concatenate([qt_ref[0], qat_ref[0]], axis=0)
    row_q = lax.broadcasted_iota(jnp.int32, qcat.shape, 0) & (LANES - 1)
    qs = [jnp.where((row_q >= h * hd) & (row_q < (h + 1) * hd), qcat, jnp.zeros((), BF16))
          for h in range(n_sub)]
    m_ref[...] = jnp.full_like(m_ref, NEG_INF)
    acc_ref[...] = jnp.zeros_like(acc_ref)
    p_ref[1] = jnp.zeros_like(p_ref[1])
    al_ref[1] = jnp.ones_like(al_ref[1])

    def scores(t, slot):
        start = pl.multiple_of(t * tq, tq)
        kc = jnp.concatenate([kp_ref[0, pl.ds(start, tq), :], ka_ref[0, pl.ds(start, tq), :]], axis=-1)
        for h in range(n_sub):
            s_ref[slot, h] = _dot(kc, qs[h])

    def softmax(slot, masked):
        for h in range(n_sub):
            s = s_ref[slot, h]
            if masked:
                key = lax.broadcasted_iota(jnp.int32, s.shape, 0)
                qry = lax.broadcasted_iota(jnp.int32, s.shape, 1)
                s = jnp.where(key <= qry, s, NEG_INF)
            m_prev = m_ref[h]
            m_next = jnp.maximum(m_prev, jnp.max(s, axis=0, keepdims=True))
            alpha = jnp.exp2(m_prev - m_next)
            p = jnp.exp2(s - m_next)
            m_ref[h] = m_next
            al_ref[slot, h] = alpha
            p_ref[slot, h] = p.astype(BF16)

    def values(t, slot):
        start = pl.multiple_of(jnp.maximum(t, 0) * tq, tq)
        for h in range(n_sub):
            v_t = jnp.concatenate([vt_ref[0, h * hd:(h + 1) * hd, pl.ds(start, tq)],
                                   jnp.ones((BF16_ROWS, tq), BF16)], axis=0)
            acc_ref[h] = al_ref[slot, h] * acc_ref[h] + _dot(v_t, p_ref[slot, h])

    def stage(t, slot):
        scores(t + 1, 1 - slot)
        softmax(slot, False)
        values(t - 1, 1 - slot)

    def body(u, c):
        stage(2 * u, 0)
        stage(2 * u + 1, 1)
        return c

    scores(0, 0)
    lax.fori_loop(0, i // 2, body, 0)

    @pl.when(i % 2 == 1)
    def _():
        stage(i - 1, 0)
        softmax(1, True)
        values(i - 1, 0)
        values(i, 1)

    @pl.when(i % 2 == 0)
    def _():
        softmax(0, True)
        values(i - 1, 1)
        values(i, 0)

    out_t = jnp.concatenate([acc_ref[h, :hd] / acc_ref[h, hd:hd + 1] for h in range(n_sub)], axis=0)
    o_ref[0] = out_t.T.astype(o_ref.dtype)


def _attn_call(qt, qat, kp, ka, vt, tq, hd):
    b, s, w = kp.shape
    n_sub = LANES // hd
    qt_spec = pl.BlockSpec((1, LANES, tq), lambda bi, hp, i: (bi, hp, i))
    k_spec = pl.BlockSpec((1, s, LANES), lambda bi, hp, i: (bi, 0, hp))
    return pl.pallas_call(
        functools.partial(_attn_kernel, tq=tq, hd=hd),
        out_shape=jax.ShapeDtypeStruct((b, s, w), BF16),
        grid=(b, w // LANES, s // tq),
        in_specs=[qt_spec, qt_spec, k_spec, k_spec,
                  pl.BlockSpec((1, LANES, s), lambda bi, hp, i: (bi, hp, 0))],
        out_specs=pl.BlockSpec((1, tq, LANES), lambda bi, hp, i: (bi, i, hp)),
        scratch_shapes=[pltpu.VMEM((n_sub, 1, tq), F32),
                        pltpu.VMEM((n_sub, hd + BF16_ROWS, tq), F32), pltpu.VMEM((2, n_sub, tq, tq), F32),
                        pltpu.VMEM((2, n_sub, tq, tq), BF16), pltpu.VMEM((2, n_sub, 1, tq), F32)],
        compiler_params=_params(3),
        name="fox_attn_prompt",
    )(qt, qat, kp, ka, vt)


def _decode_init(fn_ref, m_ref, l_ref, acc_ref, car_ref):
    m_ref[...] = jnp.full_like(m_ref, NEG_INF)
    l_ref[...] = jnp.zeros_like(l_ref)
    acc_ref[...] = jnp.zeros_like(acc_ref)
    car_ref[...] = jnp.broadcast_to(fn_ref[0], car_ref.shape)


def _decode_chunk(k_pages, v_pages, lf_pages, qcol_ref, ltri_ref, m_ref, l_ref, acc_ref, car_ref,
                  *, n_heads, hd):
    g_pages = len(k_pages)
    page = ltri_ref.shape[0]
    lf = jnp.concatenate([r[...] for r in lf_pages], axis=0)
    both = _dot(jnp.concatenate(_split3(lf), axis=0), ltri_ref[...])
    gh = g_pages * n_heads
    both = both[:gh] + both[gh:2 * gh] + both[2 * gh:]
    inner = both[:, :page]
    total = both[:, page:]
    carry = car_ref[...]
    bias = [None] * g_pages
    for r in reversed(range(g_pages)):
        bias[r] = inner[r * n_heads:(r + 1) * n_heads] + carry
        carry = carry + total[r * n_heads:(r + 1) * n_heads]
    car_ref[...] = carry

    s_rows = [[None] * n_heads for _ in range(g_pages)]
    for h in range(n_heads):
        hs = slice(h * hd, (h + 1) * hd)
        qh = qcol_ref[0, hs, :]
        for r in range(g_pages):
            s_rows[r][h] = jnp.sum(k_pages[r][hs, :] * qh, axis=0, keepdims=True)
    s = [jnp.concatenate(s_rows[r], axis=0) + bias[r] * LOG2E for r in range(g_pages)]
    m_prev = m_ref[...]
    m_next = m_prev
    for r in range(g_pages):
        m_next = jnp.maximum(m_next, s[r])
    m_next = jnp.broadcast_to(jnp.max(m_next, axis=-1, keepdims=True), m_prev.shape)
    alpha = jnp.exp2(m_prev - m_next)
    p = [jnp.exp2(s[r] - m_next) for r in range(g_pages)]
    l_ref[...] = alpha * l_ref[...] + sum(p[1:], p[0])
    for h in range(n_heads):
        hs = slice(h * hd, (h + 1) * hd)
        pv = v_pages[0][hs, :] * p[0][h:h + 1, :]
        for r in range(1, g_pages):
            pv = pv + v_pages[r][hs, :] * p[r][h:h + 1, :]
        acc_ref[hs, :] = alpha[h:h + 1, :] * acc_ref[hs, :] + pv
    m_ref[...] = m_next


def _decode_finish(q_ref, kn_ref, vn_ref, m_ref, l_ref, acc_ref, o_ref, *, n_heads, hd):
    width = n_heads * hd
    row = lax.broadcasted_iota(jnp.int32, (n_heads, width), 0)
    col = lax.broadcasted_iota(jnp.int32, (n_heads, width), 1)
    head_mask = (col >= row * hd) & (col < (row + 1) * hd)

    def head_row(x):
        return jnp.sum(jnp.where(head_mask, jnp.broadcast_to(x, (n_heads, width)), 0.0),
                       axis=0, keepdims=True)

    qbd = jnp.where(head_mask, jnp.broadcast_to(q_ref[0], (n_heads, width)), 0.0)
    s_new = jnp.sum(qbd * kn_ref[0], axis=-1, keepdims=True)
    m_old = m_ref[:, :1]
    m_fin = jnp.maximum(m_old, s_new)
    a_fin = jnp.exp2(m_old - m_fin)
    p_new = jnp.exp2(s_new - m_fin)
    l_fin = a_fin * jnp.sum(l_ref[...], axis=-1, keepdims=True) + p_new
    past = jnp.sum(acc_ref[...].T, axis=0, keepdims=True)
    out = (head_row(a_fin) * past + head_row(p_new) * vn_ref[0]) / head_row(l_fin)
    o_ref[0] = out.astype(o_ref.dtype)


def _ffn_decode_kernel(pt_ref, x_ref, g_ref, wgu_ref, wd_ref, qcol_ref, q_ref, kn_ref, vn_ref, fn_ref,
                       ltri_ref, ck_hbm, cv_hbm, clf_hbm, o_ref, a_ref,
                       act_ref, kbuf, vbuf, lfbuf, sem, m_ref, l_ref, acc_ref, car_ref,
                       *, d_ff, chunk, down_block, b0, g_pages, n_chunks, steps_per_sample, n_heads, hd):
    step = pl.program_id(0)
    n_steps = pl.num_programs(0)
    n_groups = steps_per_sample * n_chunks

    def copies(st, c, slot):
        b = b0 + st // steps_per_sample
        first = (n_groups - 1 - ((st % steps_per_sample) * n_chunks + c)) * g_pages
        out = []
        for r in range(g_pages):
            pid = pt_ref[b, first + r]
            out.append(pltpu.make_async_copy(ck_hbm.at[pid], kbuf.at[slot, r], sem.at[slot, 0]))
            out.append(pltpu.make_async_copy(cv_hbm.at[pid], vbuf.at[slot, r], sem.at[slot, 1]))
            out.append(pltpu.make_async_copy(clf_hbm.at[pid], lfbuf.at[slot, r], sem.at[slot, 2]))
        return out

    n_slots = kbuf.shape[0]
    ahead = n_slots - 1

    @pl.when(step == 0)
    def _():
        for c in range(ahead):
            for cp in copies(step, c, c % n_slots):
                cp.start()

    @pl.when(step % steps_per_sample == 0)
    def _():
        _decode_init(fn_ref, m_ref, l_ref, acc_ref, car_ref)

    def decode(c):
        slot = c % n_slots
        nxt = c + ahead
        if nxt < n_chunks:
            for cp in copies(step, nxt, nxt % n_slots):
                cp.start()
        else:
            @pl.when(step + 1 < n_steps)
            def _():
                for cp in copies(step + 1, nxt - n_chunks, nxt % n_slots):
                    cp.start()
        for cp in copies(step, c, slot):
            cp.wait()
        _decode_chunk([kbuf.at[slot, r] for r in range(g_pages)],
                      [vbuf.at[slot, r] for r in range(g_pages)],
                      [lfbuf.at[slot, r] for r in range(g_pages)],
                      qcol_ref, ltri_ref, m_ref, l_ref, acc_ref, car_ref, n_heads=n_heads, hd=hd)

    x = x_ref[...]
    h = _rms(x, g_ref[...]).astype(BF16)

    def up(c):
        a = _dot(h, wgu_ref[:, c * chunk:(c + 1) * chunk])
        b = _dot(h, wgu_ref[:, d_ff + c * chunk:d_ff + (c + 1) * chunk])
        act_ref[:, c * chunk:(c + 1) * chunk] = (jax.nn.silu(a) * b).astype(BF16)

    def down(nb):
        cols = slice(nb * down_block, (nb + 1) * down_block)
        o_ref[:, cols] = x[:, cols] + 0.5 * _dot(act_ref[...], wd_ref[:, cols])

    units = ([functools.partial(up, c) for c in range(d_ff // chunk)]
             + [functools.partial(down, nb) for nb in range(x.shape[1] // down_block)])
    chunk_at = {(c * len(units)) // n_chunks: c for c in range(n_chunks)}
    for u, unit in enumerate(units):
        if u in chunk_at:
            decode(chunk_at[u])
        unit()

    @pl.when(step % steps_per_sample == steps_per_sample - 1)
    def _():
        _decode_finish(q_ref, kn_ref, vn_ref, m_ref, l_ref, acc_ref, a_ref, n_heads=n_heads, hd=hd)


def _ffn_decode_call(x, g, wgu, wd, tm, page_table, b0, n_samples, qcol, q, kn, vn, fn,
                     cache_kt, cache_vt, cache_lf_t, g_pages, n_heads, hd):
    n, d = x.shape
    d_ff = wd.shape[0]
    chunk = 256 if d_ff % 256 == 0 else LANES
    n_steps = n // tm
    n_pages = page_table.shape[1]
    n_pool, width, page = cache_kt.shape
    assert n_steps % n_samples == 0
    steps_per_sample = n_steps // n_samples
    assert n_pages % (steps_per_sample * g_pages) == 0
    n_chunks = n_pages // (steps_per_sample * g_pages)
    down_block = 512 if d % 512 == 0 else d
    n_slots = 4
    assert n_chunks % n_slots == 0, "a chunk's ring slot must not depend on the grid step"
    assert n_chunks <= d_ff // chunk + d // down_block
    ltri = jnp.asarray(np.concatenate([np.tril(np.ones((page, page), np.float32), -1),
                                       np.ones((page, page), np.float32)], axis=1), BF16)
    sample = lambda i, pt: (b0 + i // steps_per_sample, 0, 0)
    const = lambda shape: pl.BlockSpec(shape, lambda i, pt: (0,) * len(shape), pipeline_mode=pl.Buffered(1))
    row_spec = pl.BlockSpec((1, 1, width), sample)
    hbm = pl.BlockSpec(memory_space=pl.ANY)
    return pl.pallas_call(
        functools.partial(_ffn_decode_kernel, d_ff=d_ff, chunk=chunk, down_block=down_block, b0=b0, g_pages=g_pages,
                          n_chunks=n_chunks, steps_per_sample=steps_per_sample, n_heads=n_heads, hd=hd),
        out_shape=(jax.ShapeDtypeStruct((n, d), F32), jax.ShapeDtypeStruct((n_samples, 1, width), BF16)),
        grid_spec=pltpu.PrefetchScalarGridSpec(
            num_scalar_prefetch=1,
            grid=(n_steps,),
            in_specs=[pl.BlockSpec((tm, d), lambda i, pt: (i, 0)),
                      const(g.shape), const(wgu.shape), const(wd.shape),
                      pl.BlockSpec((1, width, page), sample), row_spec, row_spec, row_spec,
                      pl.BlockSpec((1, n_heads, page), sample), const(ltri.shape), hbm, hbm, hbm],
            out_specs=(pl.BlockSpec((tm, d), lambda i, pt: (i, 0)),
                       pl.BlockSpec((1, 1, width), lambda i, pt: (i // steps_per_sample, 0, 0))),
            scratch_shapes=[pltpu.VMEM((tm, d_ff), BF16),
                            pltpu.VMEM((n_slots, g_pages, width, page), F32),
                            pltpu.VMEM((n_slots, g_pages, width, page), F32),
                            pltpu.VMEM((n_slots, g_pages, n_heads, page), F32),
                            pltpu.SemaphoreType.DMA((n_slots, 3)),
                            pltpu.VMEM((n_heads, page), F32), pltpu.VMEM((n_heads, page), F32),
                            pltpu.VMEM((width, page), F32), pltpu.VMEM((n_heads, page), F32)]),
        compiler_params=_params(1),
        name="ffn_decode",
    )(page_table, x, g, wgu, wd, qcol, q, kn, vn, fn, ltri, cache_kt, cache_vt, cache_lf_t)


def _merge_kernel(x_ref, a_ref, mg_ref, g_ref, wga_ref, wgb_ref, wpa_ref, wpg_ref, wo_ref, o_ref):
    x = x_ref[...]
    h = _rms(x, g_ref[...]).astype(BF16)
    ga = jax.nn.sigmoid(_dot(h, wga_ref[...]))
    gb = jax.nn.sigmoid(_dot(h, wgb_ref[...]))
    merged = ga * _dot(a_ref[...], wpa_ref[...]) + gb * _dot(mg_ref[...], wpg_ref[...])
    o_ref[...] = x + _dot(merged.astype(BF16), wo_ref[...])


def _merge_call(x, a, mg, g, wga, wgb, wpa, wpg, wo, tm):
    n, d = x.shape
    w = a.shape[1]
    row = lambda cols: pl.BlockSpec((tm, cols), lambda i: (i, 0))
    return pl.pallas_call(
        _merge_kernel,
        out_shape=jax.ShapeDtypeStruct((n, d), F32),
        grid=(n // tm,),
        in_specs=[row(d), row(w), row(mg.shape[1])] + [_const_spec(t.shape) for t in (g, wga, wgb, wpa, wpg, wo)],
        out_specs=row(d),
        compiler_params=_params(1),
        name="merge_out",
    )(x, a, mg, g, wga, wgb, wpa, wpg, wo)


def _ple_kernel(x_ref, p_ref, g_ref, wg_ref, wp_ref, o_ref):
    x = x_ref[...]
    gate = jax.nn.sigmoid(_dot(_rms(x, g_ref[...]).astype(BF16), wg_ref[...]))
    o_ref[...] = x + gate * _dot(p_ref[...].astype(BF16), wp_ref[...])


def _ple_call(x, p, g, wg, wp, tm):
    n, d = x.shape
    row = lambda cols: pl.BlockSpec((tm, cols), lambda i: (i, 0))
    return pl.pallas_call(
        _ple_kernel,
        out_shape=jax.ShapeDtypeStruct((n, d), F32),
        grid=(n // tm,),
        in_specs=[row(d), row(p.shape[1])] + [_const_spec(t.shape) for t in (g, wg, wp)],
        out_specs=row(d),
        compiler_params=_params(1),
        name="ple",
    )(x, p, g, wg, wp)


def _block_diag_mean(width, group):
    idx = np.arange(width) // group
    return jnp.asarray((idx[:, None] == idx[None, :]).astype(np.float32) / group, BF16)


def _forget_selectors(n_heads, hd):
    width = n_heads * hd
    selq = np.zeros((3 * LANES, width), np.float32)
    selk = np.zeros((3 * LANES, width), np.float32)
    for h in range(n_heads):
        for p in range(3):
            selq[p * LANES + h, h * hd + p] = 1.0
            selk[p * LANES + h, h * hd + 3 + p] = -1.0
            selq[n_heads, h * hd + 3 + p] = 1.0
            selk[n_heads, h * hd + p] = 1.0
    return jnp.asarray(selq.T, BF16), jnp.asarray(selk, BF16)


def kernel(x_prompt, x_sample, cache_k, cache_v, cache_logf, page_table, p_prompt, p_sample,
           ffn1_norm, ffn1_w_gu, ffn1_w_down, mix_norm, w_in, b_forget, q_norm, k_norm,
           gmlp_v_norm, w_spatial, b_spatial, w_proj_attn, w_proj_gmlp, w_out,
           ffn2_norm, ffn2_w_gu, ffn2_w_down, ple_norm, ple_w_gate, ple_w_proj):
    depth = ffn1_norm.shape[0]
    bsz, seq, d = x_prompt.shape
    db = x_sample.shape[0]
    assert x_sample.shape[1] == 1, "the sample group decodes one token per step"
    n_heads, hd = cache_k.shape[3], cache_k.shape[4]
    aw = n_heads * hd
    n_groups, chunk = w_spatial.shape[1], w_spatial.shape[2]
    gw = w_proj_gmlp.shape[1]
    gdim = gw // n_groups
    page = cache_k.shape[2]
    n_pool = cache_k.shape[1]
    assert hd == gdim and aw == gw and LANES % hd == 0 and n_heads <= LANES and page == LANES

    tm = 512
    tq = 512
    g_pages = 8
    scale = hd ** -0.5
    tri = jnp.asarray(np.tril(np.ones((tm, tm), np.float32)), BF16)
    bd = _block_diag_mean(math.gcd(aw, MXU_TILE), hd)
    selqt, selk = _forget_selectors(n_heads, hd)

    xp = x_prompt.reshape(bsz * seq, d)
    xs = x_sample.reshape(db, d)
    outs = {k: [] for k in ("kp", "vp", "fp", "ks", "vs", "fs", "gs")}
    for li in range(depth):
        bf = lambda t: t[li].astype(BF16)
        row = lambda t: t[li].reshape(1, -1).astype(F32)
        tile_h = lambda t, mult=1.0: jnp.tile(t[li].astype(F32) * mult, n_heads).reshape(1, -1)
        w1gu, w1d, w2gu, w2d = bf(ffn1_w_gu), bf(ffn1_w_down), bf(ffn2_w_gu), bf(ffn2_w_down)
        wi = w_in[li]
        o = 0
        wqkv = wi[:, o:o + 3 * aw].astype(BF16); o += 3 * aw
        wf = jnp.pad(wi[:, o:o + n_heads], ((0, 0), (0, LANES - n_heads))).astype(BF16); o += n_heads
        wugv = wi[:, o:o + 2 * gw].astype(BF16); o += 2 * gw
        wga = wi[:, o:o + d].astype(BF16); o += d
        wgb = wi[:, o:o + d].astype(BF16)
        bfg = jnp.pad(b_forget[li].astype(F32), (0, LANES - n_heads)).reshape(1, LANES)
        qg, kg, gvg = tile_h(q_norm, scale * LOG2E), tile_h(k_norm), tile_h(gmlp_v_norm)
        ws = bf(w_spatial)
        bs = jnp.repeat(b_spatial[li].astype(F32).T, gdim, axis=1)
        w00 = jnp.repeat(w_spatial[li, :, 0, 0].astype(F32), gdim).reshape(1, gw)
        b0 = jnp.repeat(b_spatial[li, :, 0].astype(F32), gdim).reshape(1, gw)
        wpa, wpg, wo = bf(w_proj_attn), bf(w_proj_gmlp), bf(w_out)
        plg, plp = bf(ple_w_gate), bf(ple_w_proj)
        mixer_w = (row(mix_norm), wqkv, wf, bfg, wugv, qg, kg, gvg, bd)

        x1s = _ffn_call(xs, row(ffn1_norm), w1gu, w1d, db)
        sconsts = mixer_w + (w00, b0)
        full = lambda cols: pl.BlockSpec((db, cols), lambda i: (0, 0))
        ssd = lambda cols, dt: jax.ShapeDtypeStruct((db, cols), dt)
        qs, ksn, vsn, lfs, gvs, mgs = pl.pallas_call(
            _mixer_sample_kernel,
            out_shape=(ssd(aw, F32), ssd(aw, F32), ssd(aw, F32), ssd(LANES, F32), ssd(gw, F32), ssd(gw, BF16)),
            grid=(1,),
            in_specs=[full(d)] + [_const_spec(t.shape) for t in sconsts],
            out_specs=tuple(full(c) for c in (aw, aw, aw, LANES, gw, gw)),
            compiler_params=_params(1),
            name="mixer_sample",
        )(x1s, *sconsts)
        fn = jnp.broadcast_to(lfs[:, :n_heads, None], (db, n_heads, page))
        qcol = jnp.broadcast_to(qs[:, :, None], (db, aw, page))
        cache_lf_t = jnp.swapaxes(cache_logf[li].astype(F32), 1, 2)
        cache_kt = jnp.transpose(cache_k[li], (0, 2, 3, 1)).reshape(n_pool, aw, page)
        cache_vt = jnp.transpose(cache_v[li], (0, 2, 3, 1)).reshape(n_pool, aw, page)
        decode_args = (qcol, qs.reshape(db, 1, aw), ksn.reshape(db, 1, aw), vsn.reshape(db, 1, aw), fn,
                       cache_kt, cache_vt, cache_lf_t, g_pages, n_heads, hd)
        half = db // 2

        n = bsz * seq
        x1, a_s0 = _ffn_decode_call(xp, row(ffn1_norm), w1gu, w1d, tm, page_table, 0, half, *decode_args)
        rowspec = lambda cols: pl.BlockSpec((tm, cols), lambda i: (i, 0))
        consts = mixer_w + (tri, selqt, selk, ws, bs)
        sds = lambda cols, dt: jax.ShapeDtypeStruct((n, cols), dt)
        tps = seq // tm
        tsd = lambda rows_, dt: jax.ShapeDtypeStruct((bsz, rows_, seq), dt)
        tspec = lambda rows_: pl.BlockSpec((1, rows_, tm), lambda i: (i // tps, 0, i % tps))
        qt, qat, k32t, kp, ka, v32t, vt, lfpt, mg = pl.pallas_call(
            functools.partial(_mixer_prompt_kernel, tiles_per_seq=tps, n_heads=n_heads, chunk=chunk),
            out_shape=(tsd(aw, BF16), tsd(aw, BF16), tsd(aw, F32), sds(aw, BF16), sds(aw, BF16),
                       tsd(aw, F32), tsd(aw, BF16), tsd(n_heads, F32), sds(gw, BF16)),
            grid=(n // tm,),
            in_specs=[rowspec(d)] + [_const_spec(t.shape) for t in consts],
            out_specs=(tspec(aw), tspec(aw), tspec(aw), rowspec(aw), rowspec(aw),
                       tspec(aw), tspec(aw), tspec(n_heads), rowspec(gw)),
            scratch_shapes=[pltpu.VMEM((1, LANES), F32)],
            compiler_params=_params(1),
            name="mixer_prompt",
        )(x1, *consts)
        r3 = lambda t: t.reshape(bsz, seq, aw)
        a = _attn_call(qt, qat, r3(kp), r3(ka), vt, tq, hd).reshape(n, aw)
        x2 = _merge_call(x1, a, mg, row(mix_norm), wga, wgb, wpa, wpg, wo, tm)
        x3, a_s1 = _ffn_decode_call(x2, row(ffn2_norm), w2gu, w2d, tm, page_table, half, db - half,
                                    *decode_args)
        xp = _ple_call(x3, p_prompt[li].reshape(n, -1), row(ple_norm), plg, plp, tm)
        outs["kp"].append(k32t.reshape(bsz, n_heads, hd, seq).transpose(0, 3, 1, 2))
        outs["vp"].append(v32t.reshape(bsz, n_heads, hd, seq).transpose(0, 3, 1, 2))
        outs["fp"].append(lfpt.transpose(0, 2, 1))

        a_s = jnp.concatenate([a_s0, a_s1], axis=0).reshape(db, aw)
        x2s = _merge_call(x1s, a_s, mgs, row(mix_norm), wga, wgb, wpa, wpg, wo, db)
        x3s = _ffn_call(x2s, row(ffn2_norm), w2gu, w2d, db)
        xs = _ple_call(x3s, p_sample[li].reshape(db, -1), row(ple_norm), plg, plp, db)
        outs["ks"].append(ksn.reshape(db, 1, n_heads, hd))
        outs["vs"].append(vsn.reshape(db, 1, n_heads, hd))
        outs["fs"].append(lfs[:, :n_heads].reshape(db, 1, n_heads))
        outs["gs"].append(gvs.reshape(db, 1, n_groups, gdim))

    st = lambda key: jnp.stack(outs[key])
    return (xp.reshape(bsz, seq, d), xs.reshape(db, 1, d), st("kp"), st("vp"), st("fp"),
            st("ks"), st("vs"), st("fs"), st("gs"))
```

```python
import functools
import math

import jax
import jax.numpy as jnp
import numpy as np
from jax import lax
from jax.experimental import pallas as pl
from jax.experimental.pallas import tpu as pltpu

EPS = 1e-6
NEG_INF = -1e30
LOG2E = 1.4426950408889634
LANES = 128
BF16_ROWS = 16
MXU_TILE = 256
F32 = jnp.float32
BF16 = jnp.bfloat16
VMEM_LIMIT = 56 * 1024 * 1024


def _dot(a, b):
    return jnp.dot(a, b, preferred_element_type=F32)


def _rms(x, g):
    ms = jnp.mean(x * x, axis=-1, keepdims=True)
    return x * lax.rsqrt(ms + EPS) * g


def _group_rms(x, g, bd):
    x2 = (x * x).astype(BF16)
    w = bd.shape[0]
    ms = jnp.concatenate([_dot(x2[:, i * w:(i + 1) * w], bd) for i in range(x.shape[1] // w)], axis=-1)
    return x * lax.rsqrt(ms + EPS) * g


def _split3(x):
    hi = x.astype(BF16)
    r = x - hi.astype(F32)
    mid = r.astype(BF16)
    lo = (r - mid.astype(F32)).astype(BF16)
    return hi, mid, lo


def _const_spec(shape):
    nd = len(shape)
    return pl.BlockSpec(shape, lambda *_: (0,) * nd, pipeline_mode=pl.Buffered(1))


def _params(n_axes):
    return pltpu.CompilerParams(dimension_semantics=("arbitrary",) * n_axes,
                                vmem_limit_bytes=VMEM_LIMIT)


def _ffn_chunk(d_ff):
    return MXU_TILE if d_ff % MXU_TILE == 0 else LANES


def _ffn_kernel(x_ref, g_ref, wgu_ref, wd_ref, o_ref, act_ref, *, d_ff, chunk):
    x = x_ref[...]
    h = _rms(x, g_ref[...]).astype(BF16)
    for c in range(d_ff // chunk):
        a = _dot(h, wgu_ref[:, c * chunk:(c + 1) * chunk])
        b = _dot(h, wgu_ref[:, d_ff + c * chunk:d_ff + (c + 1) * chunk])
        act_ref[:, c * chunk:(c + 1) * chunk] = (jax.nn.silu(a) * b).astype(BF16)
    o_ref[...] = x + 0.5 * _dot(act_ref[...], wd_ref[...])


def _ffn_call(x, g, wgu, wd, tm):
    n, d = x.shape
    d_ff = wd.shape[0]
    chunk = _ffn_chunk(d_ff)
    return pl.pallas_call(
        functools.partial(_ffn_kernel, d_ff=d_ff, chunk=chunk),
        out_shape=jax.ShapeDtypeStruct((n, d), F32),
        grid=(n // tm,),
        in_specs=[pl.BlockSpec((tm, d), lambda i: (i, 0)),
                  _const_spec(g.shape), _const_spec(wgu.shape), _const_spec(wd.shape)],
        out_specs=pl.BlockSpec((tm, d), lambda i: (i, 0)),
        scratch_shapes=[pltpu.VMEM((tm, d_ff), BF16)],
        compiler_params=_params(1),
        name="ffn",
    )(x, g, wgu, wd)


def _mixer_common(x_ref, g_ref, wqkv_ref, wf_ref, bf_ref, wugv_ref, qg_ref, kg_ref, gvg_ref, bd_ref):
    h = _rms(x_ref[...], g_ref[...]).astype(BF16)
    aw = qg_ref.shape[1]
    qkv = _dot(h, wqkv_ref[...])
    q = _group_rms(qkv[:, :aw], qg_ref[...], bd_ref[...])
    k = _group_rms(qkv[:, aw:2 * aw], kg_ref[...], bd_ref[...])
    v = qkv[:, 2 * aw:]
    logf = jax.nn.log_sigmoid(_dot(h, wf_ref[...]) + bf_ref[...])
    ugv = _dot(h, wugv_ref[...])
    gw = gvg_ref.shape[1]
    u = jax.nn.gelu(ugv[:, :gw])
    gv = _group_rms(jax.nn.gelu(ugv[:, gw:]), gvg_ref[...], bd_ref[...])
    return q, k, v, logf, u, gv


def _mixer_prompt_kernel(x_ref, g_ref, wqkv_ref, wf_ref, bf_ref, wugv_ref, qg_ref, kg_ref, gvg_ref,
                         bd_ref, tri_ref, selqt_ref, selk_ref, ws_ref, bs_ref,
                         qt_ref, qat_ref, k_ref, kp_ref, ka_ref, v_ref, vt_ref, lf_ref, mg_ref,
                         carry_ref, *, tiles_per_seq, n_heads, chunk):
    q, k, v, logf, u, gv = _mixer_common(x_ref, g_ref, wqkv_ref, wf_ref, bf_ref, wugv_ref,
                                         qg_ref, kg_ref, gvg_ref, bd_ref)
    tm = q.shape[0]
    qt_ref[0] = q.T.astype(BF16)
    k_ref[0] = k.T
    kp_ref[...] = k.astype(BF16)
    v_t = v.T
    v_ref[0] = v_t
    vt_ref[0] = v_t.astype(BF16)
    lf_ref[0] = logf.T[:n_heads]

    @pl.when(pl.program_id(0) % tiles_per_seq == 0)
    def _():
        carry_ref[...] = jnp.zeros_like(carry_ref)

    lane = lax.broadcasted_iota(jnp.int32, logf.shape, 1)
    lf = jnp.where(lane < n_heads, logf, 0.0)
    pieces = jnp.concatenate(_split3(lf), axis=-1)
    cs = _dot(tri_ref[...], pieces)
    f_cum = cs[:, :LANES] + cs[:, LANES:2 * LANES] + cs[:, 2 * LANES:] + carry_ref[...]
    carry_ref[...] = f_cum[tm - 1:tm, :]
    f_ext = jnp.where(lane == n_heads, 1.0, f_cum * LOG2E)
    ka_ref[...] = _dot(jnp.concatenate(_split3(f_ext), axis=-1), selk_ref[...]).astype(BF16)
    fpt = jnp.concatenate(_split3(f_ext.T), axis=0)
    qat_ref[0] = _dot(selqt_ref[...], fpt).astype(BF16)

    gvb = gv.astype(BF16)
    r_i = lax.broadcasted_iota(jnp.int32, (chunk, chunk), 0)
    c_i = lax.broadcasted_iota(jnp.int32, (chunk, chunk), 1)
    lane_c = lax.broadcasted_iota(jnp.int32, (chunk, LANES), 1)
    n_groups = ws_ref.shape[0]
    gdim = gvb.shape[1] // n_groups
    per_blk = LANES // gdim
    ws = [jnp.where(c_i <= r_i, ws_ref[g], jnp.zeros((), BF16)) for g in range(n_groups)]
    for c in range(tm // chunk):
        rows = slice(c * chunk, (c + 1) * chunk)
        for jb in range(gvb.shape[1] // LANES):
            cols = slice(jb * LANES, (jb + 1) * LANES)
            blk = gvb[rows, cols]
            sp = _dot(ws[jb * per_blk], blk)
            for gi in range(1, per_blk):
                sp = jnp.where(lane_c >= gi * gdim, _dot(ws[jb * per_blk + gi], blk), sp)
            mg_ref[rows, cols] = (u[rows, cols] * (sp + bs_ref[:, cols])).astype(BF16)


def _mixer_sample_kernel(x_ref, g_ref, wqkv_ref, wf_ref, bf_ref, wugv_ref, qg_ref, kg_ref, gvg_ref,
                         bd_ref, w00_ref, b0_ref,
                         q_ref, k_ref, v_ref, lf_ref, gv_ref, mg_ref):
    q, k, v, logf, u, gv = _mixer_common(x_ref, g_ref, wqkv_ref, wf_ref, bf_ref, wugv_ref,
                                         qg_ref, kg_ref, gvg_ref, bd_ref)
    q_ref[...] = q
    k_ref[...] = k
    v_ref[...] = v
    lf_ref[...] = logf
    gv_ref[...] = gv
    mg_ref[...] = (u * (gv * w00_ref[...] + b0_ref[...])).astype(BF16)


def _attn_kernel(qt_ref, qat_ref, kp_ref, ka_ref, vt_ref, o_ref, m_ref, acc_ref,
                 s_ref, p_ref, al_ref, bm_ref, *, tq, hd):
    i = pl.program_id(2)
    n_sub = LANES // hd
    qcat = jnp.concatenate([qt_ref[0], qat_ref[0]], axis=0)
    row_q = lax.broadcasted_iota(jnp.int32, qcat.shape, 0) & (LANES - 1)
    qs = [jnp.where((row_q >= h * hd) & (row_q < (h + 1) * hd), qcat, jnp.zeros((), BF16))
          for h in range(n_sub)]
    m_ref[...] = jnp.full_like(m_ref, NEG_INF)
    acc_ref[...] = jnp.zeros_like(acc_ref)
    p_ref[1] = jnp.zeros_like(p_ref[1])
    al_ref[1] = jnp.ones_like(al_ref[1])

    def scores(t, slot):
        start = pl.multiple_of(t * tq, tq)
        kc = jnp.concatenate([kp_ref[0, pl.ds(start, tq), :], ka_ref[0, pl.ds(start, tq), :]], axis=-1)
        for h in range(n_sub):
            s = _dot(kc, qs[h])
            s_ref[slot, h] = s
            bm_ref[slot, h] = jnp.max(s, axis=0, keepdims=True)

    def softmax(slot, masked):
        for h in range(n_sub):
            s = s_ref[slot, h]
            if masked:
                key = lax.broadcasted_iota(jnp.int32, s.shape, 0)
                qry = lax.broadcasted_iota(jnp.int32, s.shape, 1)
                s = jnp.where(key <= qry, s, NEG_INF)
                blk_max = jnp.max(s, axis=0, keepdims=True)
            else:
                blk_max = bm_ref[slot, h]
            m_prev = m_ref[h]
            m_next = jnp.maximum(m_prev, blk_max)
            alpha = jnp.exp2(m_prev - m_next)
            p = jnp.exp2(s - m_next)
            m_ref[h] = m_next
            al_ref[slot, h] = alpha
            p_ref[slot, h] = p.astype(BF16)

    def values(t, slot):
        start = pl.multiple_of(jnp.maximum(t, 0) * tq, tq)
        for h in range(n_sub):
            v_t = jnp.concatenate([vt_ref[0, h * hd:(h + 1) * hd, pl.ds(start, tq)],
                                   jnp.ones((BF16_ROWS, tq), BF16)], axis=0)
            acc_ref[h] = al_ref[slot, h] * acc_ref[h] + _dot(v_t, p_ref[slot, h])

    UNROLL = 2

    def stage(t, slot):
        scores(t + 1, 1 - slot)
        softmax(slot, False)
        values(t - 1, 1 - slot)

    def body(u, c):
        for k in range(UNROLL):
            stage(UNROLL * u + k, k % 2)
        return c

    scores(0, 0)
    lax.fori_loop(0, i // UNROLL, body, 0)

    for rem in range(UNROLL):
        @pl.when(i % UNROLL == rem)
        def _(rem=rem):
            for k in range(rem):
                stage(i - rem + k, k % 2)
            softmax(rem % 2, True)
            values(i - 1, 1 - rem % 2)
            values(i, rem % 2)

    out_t = jnp.concatenate([acc_ref[h, :hd] / acc_ref[h, hd:hd + 1] for h in range(n_sub)], axis=0)
    o_ref[0] = out_t.T.astype(o_ref.dtype)


def _attn_call(qt, qat, kp, ka, vt, tq, hd):
    b, s, w = kp.shape
    n_sub = LANES // hd
    qt_spec = pl.BlockSpec((1, LANES, tq), lambda bi, hp, i: (bi, hp, i))
    k_spec = pl.BlockSpec((1, s, LANES), lambda bi, hp, i: (bi, 0, hp))
    return pl.pallas_call(
        functools.partial(_attn_kernel, tq=tq, hd=hd),
        out_shape=jax.ShapeDtypeStruct((b, s, w), BF16),
        grid=(b, w // LANES, s // tq),
        in_specs=[qt_spec, qt_spec, k_spec, k_spec,
                  pl.BlockSpec((1, LANES, s), lambda bi, hp, i: (bi, hp, 0))],
        out_specs=pl.BlockSpec((1, tq, LANES), lambda bi, hp, i: (bi, i, hp)),
        scratch_shapes=[pltpu.VMEM((n_sub, 1, tq), F32),
                        pltpu.VMEM((n_sub, hd + BF16_ROWS, tq), F32), pltpu.VMEM((2, n_sub, tq, tq), F32),
                        pltpu.VMEM((2, n_sub, tq, tq), BF16), pltpu.VMEM((2, n_sub, 1, tq), F32),
                        pltpu.VMEM((2, n_sub, 1, tq), F32)],
        compiler_params=_params(3),
        name="fox_attn_prompt",
    )(qt, qat, kp, ka, vt)


def _decode_init(fn_ref, m_ref, l_ref, acc_ref, car_ref):
    m_ref[...] = jnp.full_like(m_ref, NEG_INF)
    l_ref[...] = jnp.zeros_like(l_ref)
    acc_ref[...] = jnp.zeros_like(acc_ref)
    car_ref[...] = jnp.broadcast_to(fn_ref[0], car_ref.shape)


def _decode_chunk(k_pages, v_pages, lf_pages, qcol_ref, ltri_ref, m_ref, l_ref, acc_ref, car_ref,
                  *, n_heads, hd):
    g_pages = len(k_pages)
    page = ltri_ref.shape[0]
    lf = jnp.concatenate([r[...] for r in lf_pages], axis=0)
    both = _dot(jnp.concatenate(_split3(lf), axis=0), ltri_ref[...])
    gh = g_pages * n_heads
    both = both[:gh] + both[gh:2 * gh] + both[2 * gh:]
    inner = both[:, :page]
    total = both[:, page:]
    carry = car_ref[...]
    bias = [None] * g_pages
    for r in reversed(range(g_pages)):
        bias[r] = inner[r * n_heads:(r + 1) * n_heads] + carry
        carry = carry + total[r * n_heads:(r + 1) * n_heads]
    car_ref[...] = carry

    s_rows = [[None] * n_heads for _ in range(g_pages)]
    for h in range(n_heads):
        hs = slice(h * hd, (h + 1) * hd)
        qh = qcol_ref[0, hs, :]
        for r in range(g_pages):
            s_rows[r][h] = jnp.sum(k_pages[r][hs, :] * qh, axis=0, keepdims=True)
    s = [jnp.concatenate(s_rows[r], axis=0) + bias[r] * LOG2E for r in range(g_pages)]
    m_prev = m_ref[...]
    m_next = m_prev
    for r in range(g_pages):
        m_next = jnp.maximum(m_next, s[r])
    m_next = jnp.broadcast_to(jnp.max(m_next, axis=-1, keepdims=True), m_prev.shape)
    alpha = jnp.exp2(m_prev - m_next)
    p = [jnp.exp2(s[r] - m_next) for r in range(g_pages)]
    l_ref[...] = alpha * l_ref[...] + sum(p[1:], p[0])
    for h in range(n_heads):
        hs = slice(h * hd, (h + 1) * hd)
        pv = v_pages[0][hs, :] * p[0][h:h + 1, :]
        for r in range(1, g_pages):
            pv = pv + v_pages[r][hs, :] * p[r][h:h + 1, :]
        acc_ref[hs, :] = alpha[h:h + 1, :] * acc_ref[hs, :] + pv
    m_ref[...] = m_next


def _decode_finish(q_ref, kn_ref, vn_ref, m_ref, l_ref, acc_ref, o_ref, *, n_heads, hd):
    width = n_heads * hd
    row = lax.broadcasted_iota(jnp.int32, (n_heads, width), 0)
    col = lax.broadcasted_iota(jnp.int32, (n_heads, width), 1)
    head_mask = (col >= row * hd) & (col < (row + 1) * hd)

    def head_row(x):
        return jnp.sum(jnp.where(head_mask, jnp.broadcast_to(x, (n_heads, width)), 0.0),
                       axis=0, keepdims=True)

    qbd = jnp.where(head_mask, jnp.broadcast_to(q_ref[0], (n_heads, width)), 0.0)
    s_new = jnp.sum(qbd * kn_ref[0], axis=-1, keepdims=True)
    m_old = m_ref[:, :1]
    m_fin = jnp.maximum(m_old, s_new)
    a_fin = jnp.exp2(m_old - m_fin)
    p_new = jnp.exp2(s_new - m_fin)
    l_fin = a_fin * jnp.sum(l_ref[...], axis=-1, keepdims=True) + p_new
    past = jnp.sum(acc_ref[...].T, axis=0, keepdims=True)
    out = (head_row(a_fin) * past + head_row(p_new) * vn_ref[0]) / head_row(l_fin)
    o_ref[0] = out.astype(o_ref.dtype)


def _ffn_decode_kernel(pt_ref, x_ref, g_ref, wgu_ref, wd_ref, qcol_ref, q_ref, kn_ref, vn_ref, fn_ref,
                       ltri_ref, ck_hbm, cv_hbm, clf_hbm, *refs,
                       d_ff, chunk, down_block, b0, g_pages, n_chunks, steps_per_sample, n_heads, hd, with_ple):
    if with_ple:
        pe_ref, pg_ref, wpg_ref, wpp_ref = refs[:4]
        refs = refs[4:]
    o_ref, a_ref, act_ref, kbuf, vbuf, lfbuf, sem, m_ref, l_ref, acc_ref, car_ref = refs
    step = pl.program_id(0)
    n_steps = pl.num_programs(0)
    n_groups = steps_per_sample * n_chunks

    def copies(st, c, slot):
        b = b0 + st // steps_per_sample
        first = (n_groups - 1 - ((st % steps_per_sample) * n_chunks + c)) * g_pages
        out = []
        for r in range(g_pages):
            pid = pt_ref[b, first + r]
            out.append(pltpu.make_async_copy(ck_hbm.at[pid], kbuf.at[slot, r], sem.at[slot, 0]))
            out.append(pltpu.make_async_copy(cv_hbm.at[pid], vbuf.at[slot, r], sem.at[slot, 1]))
            out.append(pltpu.make_async_copy(clf_hbm.at[pid], lfbuf.at[slot, r], sem.at[slot, 2]))
        return out

    n_slots = kbuf.shape[0]
    ahead = n_slots - 1

    @pl.when(step == 0)
    def _():
        for c in range(ahead):
            for cp in copies(step, c, c % n_slots):
                cp.start()

    @pl.when(step % steps_per_sample == 0)
    def _():
        _decode_init(fn_ref, m_ref, l_ref, acc_ref, car_ref)

    def decode(c):
        slot = c % n_slots
        nxt = c + ahead
        if nxt < n_chunks:
            for cp in copies(step, nxt, nxt % n_slots):
                cp.start()
        else:
            @pl.when(step + 1 < n_steps)
            def _():
                for cp in copies(step + 1, nxt - n_chunks, nxt % n_slots):
                    cp.start()
        for cp in copies(step, c, slot):
            cp.wait()
        _decode_chunk([kbuf.at[slot, r] for r in range(g_pages)],
                      [vbuf.at[slot, r] for r in range(g_pages)],
                      [lfbuf.at[slot, r] for r in range(g_pages)],
                      qcol_ref, ltri_ref, m_ref, l_ref, acc_ref, car_ref, n_heads=n_heads, hd=hd)

    x = x_ref[...]
    h = _rms(x, g_ref[...]).astype(BF16)

    def up(c):
        a = _dot(h, wgu_ref[:, c * chunk:(c + 1) * chunk])
        b = _dot(h, wgu_ref[:, d_ff + c * chunk:d_ff + (c + 1) * chunk])
        act_ref[:, c * chunk:(c + 1) * chunk] = (jax.nn.silu(a) * b).astype(BF16)

    def down(nb):
        cols = slice(nb * down_block, (nb + 1) * down_block)
        o_ref[:, cols] = x[:, cols] + 0.5 * _dot(act_ref[...], wd_ref[:, cols])

    def ple():
        x3 = o_ref[...]
        gate = jax.nn.sigmoid(_dot(_rms(x3, pg_ref[...]).astype(BF16), wpg_ref[...]))
        o_ref[...] = x3 + gate * _dot(pe_ref[...].astype(BF16), wpp_ref[...])

    units = ([functools.partial(up, c) for c in range(d_ff // chunk)]
             + [functools.partial(down, nb) for nb in range(x.shape[1] // down_block)]
             + ([ple] if with_ple else []))
    for u, unit in enumerate(units):
        for c in range(n_chunks):
            if (c * len(units)) // n_chunks == u:
                decode(c)
        unit()

    @pl.when(step % steps_per_sample == steps_per_sample - 1)
    def _():
        _decode_finish(q_ref, kn_ref, vn_ref, m_ref, l_ref, acc_ref, a_ref, n_heads=n_heads, hd=hd)


def _ffn_decode_call(x, g, wgu, wd, tm, page_table, b0, n_samples, qcol, q, kn, vn, fn,
                     cache_kt, cache_vt, cache_lf_t, g_pages, n_heads, hd, ple=None):
    n, d = x.shape
    d_ff = wd.shape[0]
    chunk = _ffn_chunk(d_ff)
    n_steps = n // tm
    n_pages = page_table.shape[1]
    n_pool, width, page = cache_kt.shape
    assert n_steps % n_samples == 0
    steps_per_sample = n_steps // n_samples
    assert n_pages % (steps_per_sample * g_pages) == 0
    n_chunks = n_pages // (steps_per_sample * g_pages)
    down_block = 512 if d % 512 == 0 else d
    n_slots = 4
    assert n_chunks % n_slots == 0, "a chunk's ring slot must not depend on the grid step"
    ltri = jnp.asarray(np.concatenate([np.tril(np.ones((page, page), np.float32), -1),
                                       np.ones((page, page), np.float32)], axis=1), BF16)
    sample = lambda i, pt: (b0 + i // steps_per_sample, 0, 0)
    const = lambda shape: pl.BlockSpec(shape, lambda i, pt: (0,) * len(shape), pipeline_mode=pl.Buffered(1))
    row_spec = pl.BlockSpec((1, 1, width), sample)
    hbm = pl.BlockSpec(memory_space=pl.ANY)
    ple_specs, ple_args = [], ()
    if ple is not None:
        ple_args = tuple(ple)
        ple_specs = ([pl.BlockSpec((tm, ple[0].shape[1]), lambda i, pt: (i, 0))]
                     + [const(t.shape) for t in ple[1:]])
    return pl.pallas_call(
        functools.partial(_ffn_decode_kernel, d_ff=d_ff, chunk=chunk, down_block=down_block, b0=b0, g_pages=g_pages,
                          n_chunks=n_chunks, steps_per_sample=steps_per_sample, n_heads=n_heads, hd=hd,
                          with_ple=ple is not None),
        out_shape=(jax.ShapeDtypeStruct((n, d), F32), jax.ShapeDtypeStruct((n_samples, 1, width), BF16)),
        grid_spec=pltpu.PrefetchScalarGridSpec(
            num_scalar_prefetch=1,
            grid=(n_steps,),
            in_specs=[pl.BlockSpec((tm, d), lambda i, pt: (i, 0)),
                      const(g.shape), const(wgu.shape), const(wd.shape),
                      pl.BlockSpec((1, width, page), sample), row_spec, row_spec, row_spec,
                      pl.BlockSpec((1, n_heads, page), sample), const(ltri.shape), hbm, hbm, hbm] + ple_specs,
            out_specs=(pl.BlockSpec((tm, d), lambda i, pt: (i, 0)),
                       pl.BlockSpec((1, 1, width), lambda i, pt: (i // steps_per_sample, 0, 0))),
            scratch_shapes=[pltpu.VMEM((tm, d_ff), BF16),
                            pltpu.VMEM((n_slots, g_pages, width, page), F32),
                            pltpu.VMEM((n_slots, g_pages, width, page), F32),
                            pltpu.VMEM((n_slots, g_pages, n_heads, page), F32),
                            pltpu.SemaphoreType.DMA((n_slots, 3)),
                            pltpu.VMEM((n_heads, page), F32), pltpu.VMEM((n_heads, page), F32),
                            pltpu.VMEM((width, page), F32), pltpu.VMEM((n_heads, page), F32)]),
        compiler_params=_params(1),
        name="ffn_decode",
    )(page_table, x, g, wgu, wd, qcol, q, kn, vn, fn, ltri, cache_kt, cache_vt, cache_lf_t, *ple_args)


def _merge_kernel(x_ref, a_ref, mg_ref, g_ref, wga_ref, wgb_ref, wpa_ref, wpg_ref, wo_ref, o_ref):
    x = x_ref[...]
    h = _rms(x, g_ref[...]).astype(BF16)
    ga = jax.nn.sigmoid(_dot(h, wga_ref[...]))
    gb = jax.nn.sigmoid(_dot(h, wgb_ref[...]))
    merged = ga * _dot(a_ref[...], wpa_ref[...]) + gb * _dot(mg_ref[...], wpg_ref[...])
    o_ref[...] = x + _dot(merged.astype(BF16), wo_ref[...])


def _merge_call(x, a, mg, g, wga, wgb, wpa, wpg, wo, tm):
    n, d = x.shape
    w = a.shape[1]
    row = lambda cols: pl.BlockSpec((tm, cols), lambda i: (i, 0))
    return pl.pallas_call(
        _merge_kernel,
        out_shape=jax.ShapeDtypeStruct((n, d), F32),
        grid=(n // tm,),
        in_specs=[row(d), row(w), row(mg.shape[1])] + [_const_spec(t.shape) for t in (g, wga, wgb, wpa, wpg, wo)],
        out_specs=row(d),
        compiler_params=_params(1),
        name="merge_out",
    )(x, a, mg, g, wga, wgb, wpa, wpg, wo)


def _ple_kernel(x_ref, p_ref, g_ref, wg_ref, wp_ref, o_ref):
    x = x_ref[...]
    gate = jax.nn.sigmoid(_dot(_rms(x, g_ref[...]).astype(BF16), wg_ref[...]))
    o_ref[...] = x + gate * _dot(p_ref[...].astype(BF16), wp_ref[...])


def _ple_call(x, p, g, wg, wp, tm):
    n, d = x.shape
    row = lambda cols: pl.BlockSpec((tm, cols), lambda i: (i, 0))
    return pl.pallas_call(
        _ple_kernel,
        out_shape=jax.ShapeDtypeStruct((n, d), F32),
        grid=(n // tm,),
        in_specs=[row(d), row(p.shape[1])] + [_const_spec(t.shape) for t in (g, wg, wp)],
        out_specs=row(d),
        compiler_params=_params(1),
        name="ple",
    )(x, p, g, wg, wp)


def _block_diag_mean(width, group):
    idx = np.arange(width) // group
    return jnp.asarray((idx[:, None] == idx[None, :]).astype(np.float32) / group, BF16)


def _forget_selectors(n_heads, hd):
    width = n_heads * hd
    selq = np.zeros((3 * LANES, width), np.float32)
    selk = np.zeros((3 * LANES, width), np.float32)
    for h in range(n_heads):
        for p in range(3):
            selq[p * LANES + h, h * hd + p] = 1.0
            selk[p * LANES + h, h * hd + 3 + p] = -1.0
            selq[n_heads, h * hd + 3 + p] = 1.0
            selk[n_heads, h * hd + p] = 1.0
    return jnp.asarray(selq.T, BF16), jnp.asarray(selk, BF16)


def kernel(x_prompt, x_sample, cache_k, cache_v, cache_logf, page_table, p_prompt, p_sample,
           ffn1_norm, ffn1_w_gu, ffn1_w_down, mix_norm, w_in, b_forget, q_norm, k_norm,
           gmlp_v_norm, w_spatial, b_spatial, w_proj_attn, w_proj_gmlp, w_out,
           ffn2_norm, ffn2_w_gu, ffn2_w_down, ple_norm, ple_w_gate, ple_w_proj):
    depth = ffn1_norm.shape[0]
    bsz, seq, d = x_prompt.shape
    db = x_sample.shape[0]
    assert x_sample.shape[1] == 1, "the sample group decodes one token per step"
    n_heads, hd = cache_k.shape[3], cache_k.shape[4]
    aw = n_heads * hd
    n_groups, chunk = w_spatial.shape[1], w_spatial.shape[2]
    gw = w_proj_gmlp.shape[1]
    gdim = gw // n_groups
    page = cache_k.shape[2]
    n_pool = cache_k.shape[1]
    assert hd == gdim and aw == gw and LANES % hd == 0 and n_heads <= LANES and page == LANES

    tm = 512
    tq = 512
    g_pages = 8
    scale = hd ** -0.5
    tri = jnp.asarray(np.tril(np.ones((tm, tm), np.float32)), BF16)
    bd = _block_diag_mean(math.gcd(aw, MXU_TILE), hd)
    selqt, selk = _forget_selectors(n_heads, hd)

    xp = x_prompt.reshape(bsz * seq, d)
    xs = x_sample.reshape(db, d)
    outs = {k: [] for k in ("kp", "vp", "fp", "ks", "vs", "fs", "gs")}
    for li in range(depth):
        bf = lambda t: t[li].astype(BF16)
        row = lambda t: t[li].reshape(1, -1).astype(F32)
        tile_h = lambda t, mult=1.0: jnp.tile(t[li].astype(F32) * mult, n_heads).reshape(1, -1)
        w1gu, w1d, w2gu, w2d = bf(ffn1_w_gu), bf(ffn1_w_down), bf(ffn2_w_gu), bf(ffn2_w_down)
        wi = w_in[li]
        o = 0
        wqkv = wi[:, o:o + 3 * aw].astype(BF16); o += 3 * aw
        wf = jnp.pad(wi[:, o:o + n_heads], ((0, 0), (0, LANES - n_heads))).astype(BF16); o += n_heads
        wugv = wi[:, o:o + 2 * gw].astype(BF16); o += 2 * gw
        wga = wi[:, o:o + d].astype(BF16); o += d
        wgb = wi[:, o:o + d].astype(BF16)
        bfg = jnp.pad(b_forget[li].astype(F32), (0, LANES - n_heads)).reshape(1, LANES)
        qg, kg, gvg = tile_h(q_norm, scale * LOG2E), tile_h(k_norm), tile_h(gmlp_v_norm)
        ws = bf(w_spatial)
        bs = jnp.repeat(b_spatial[li].astype(F32).T, gdim, axis=1)
        w00 = jnp.repeat(w_spatial[li, :, 0, 0].astype(F32), gdim).reshape(1, gw)
        b0 = jnp.repeat(b_spatial[li, :, 0].astype(F32), gdim).reshape(1, gw)
        wpa, wpg, wo = bf(w_proj_attn), bf(w_proj_gmlp), bf(w_out)
        plg, plp = bf(ple_w_gate), bf(ple_w_proj)
        mixer_w = (row(mix_norm), wqkv, wf, bfg, wugv, qg, kg, gvg, bd)

        x1s = _ffn_call(xs, row(ffn1_norm), w1gu, w1d, db)
        sconsts = mixer_w + (w00, b0)
        full = lambda cols: pl.BlockSpec((db, cols), lambda i: (0, 0))
        ssd = lambda cols, dt: jax.ShapeDtypeStruct((db, cols), dt)
        qs, ksn, vsn, lfs, gvs, mgs = pl.pallas_call(
            _mixer_sample_kernel,
            out_shape=(ssd(aw, F32), ssd(aw, F32), ssd(aw, F32), ssd(LANES, F32), ssd(gw, F32), ssd(gw, BF16)),
            grid=(1,),
            in_specs=[full(d)] + [_const_spec(t.shape) for t in sconsts],
            out_specs=tuple(full(c) for c in (aw, aw, aw, LANES, gw, gw)),
            compiler_params=_params(1),
            name="mixer_sample",
        )(x1s, *sconsts)
        fn = jnp.broadcast_to(lfs[:, :n_heads, None], (db, n_heads, page))
        qcol = jnp.broadcast_to(qs[:, :, None], (db, aw, page))
        cache_lf_t = jnp.swapaxes(cache_logf[li].astype(F32), 1, 2)
        cache_kt = jnp.transpose(cache_k[li], (0, 2, 3, 1)).reshape(n_pool, aw, page)
        cache_vt = jnp.transpose(cache_v[li], (0, 2, 3, 1)).reshape(n_pool, aw, page)
        decode_args = (qcol, qs.reshape(db, 1, aw), ksn.reshape(db, 1, aw), vsn.reshape(db, 1, aw), fn,
                       cache_kt, cache_vt, cache_lf_t, g_pages, n_heads, hd)
        half = db // 2

        n = bsz * seq
        x1, a_s0 = _ffn_decode_call(xp, row(ffn1_norm), w1gu, w1d, tm, page_table, 0, half, *decode_args)
        rowspec = lambda cols: pl.BlockSpec((tm, cols), lambda i: (i, 0))
        consts = mixer_w + (tri, selqt, selk, ws, bs)
        sds = lambda cols, dt: jax.ShapeDtypeStruct((n, cols), dt)
        tps = seq // tm
        tsd = lambda rows_, dt: jax.ShapeDtypeStruct((bsz, rows_, seq), dt)
        tspec = lambda rows_: pl.BlockSpec((1, rows_, tm), lambda i: (i // tps, 0, i % tps))
        qt, qat, k32t, kp, ka, v32t, vt, lfpt, mg = pl.pallas_call(
            functools.partial(_mixer_prompt_kernel, tiles_per_seq=tps, n_heads=n_heads, chunk=chunk),
            out_shape=(tsd(aw, BF16), tsd(aw, BF16), tsd(aw, F32), sds(aw, BF16), sds(aw, BF16),
                       tsd(aw, F32), tsd(aw, BF16), tsd(n_heads, F32), sds(gw, BF16)),
            grid=(n // tm,),
            in_specs=[rowspec(d)] + [_const_spec(t.shape) for t in consts],
            out_specs=(tspec(aw), tspec(aw), tspec(aw), rowspec(aw), rowspec(aw),
                       tspec(aw), tspec(aw), tspec(n_heads), rowspec(gw)),
            scratch_shapes=[pltpu.VMEM((1, LANES), F32)],
            compiler_params=_params(1),
            name="mixer_prompt",
        )(x1, *consts)
        r3 = lambda t: t.reshape(bsz, seq, aw)
        a = _attn_call(qt, qat, r3(kp), r3(ka), vt, tq, hd).reshape(n, aw)
        x2 = _merge_call(x1, a, mg, row(mix_norm), wga, wgb, wpa, wpg, wo, tm)
        xp, a_s1 = _ffn_decode_call(x2, row(ffn2_norm), w2gu, w2d, tm, page_table, half, db - half,
                                    *decode_args,
                                    ple=(p_prompt[li].reshape(n, -1), row(ple_norm), plg, plp))
        outs["kp"].append(k32t.reshape(bsz, n_heads, hd, seq).transpose(0, 3, 1, 2))
        outs["vp"].append(v32t.reshape(bsz, n_heads, hd, seq).transpose(0, 3, 1, 2))
        outs["fp"].append(lfpt.transpose(0, 2, 1))

        a_s = jnp.concatenate([a_s0, a_s1], axis=0).reshape(db, aw)
        x2s = _merge_call(x1s, a_s, mgs, row(mix_norm), wga, wgb, wpa, wpg, wo, db)
        x3s = _ffn_call(x2s, row(ffn2_norm), w2gu, w2d, db)
        xs = _ple_call(x3s, p_sample[li].reshape(db, -1), row(ple_norm), plg, plp, db)
        outs["ks"].append(ksn.reshape(db, 1, n_heads, hd))
        outs["vs"].append(vsn.reshape(db, 1, n_heads, hd))
        outs["fs"].append(lfs[:, :n_heads].reshape(db, 1, n_heads))
        outs["gs"].append(gvs.reshape(db, 1, n_groups, gdim))

    st = lambda key: jnp.stack(outs[key])
    return (xp.reshape(bsz, seq, d), xs.reshape(db, 1, d), st("kp"), st("vp"), st("fp"),
            st("ks"), st("vs"), st("fs"), st("gs"))
```

```python
import functools
import math
from typing import NamedTuple

import jax
import jax.numpy as jnp
import numpy as np
from jax import lax
from jax.experimental import pallas as pl
from jax.experimental.pallas import tpu as pltpu

EPS = 1e-6
NEG_INF = -1e30
LOG2E = 1.4426950408889634
LANES = 128
BF16_ROWS = 16
MXU_TILE = 256
PIECE_STRIDE = 16
CACHE_RING_SLOTS = 4
F32 = jnp.float32
BF16 = jnp.bfloat16
VMEM_LIMIT = 56 * 1024 * 1024


def _dot(a, b):
    return jnp.dot(a, b, preferred_element_type=F32)


def _rms(x, g):
    ms = jnp.mean(x * x, axis=-1, keepdims=True)
    return x * lax.rsqrt(ms + EPS) * g


def _group_rms(x, g, bd):
    x2 = (x * x).astype(BF16)
    w = bd.shape[0]
    ms = jnp.concatenate([_dot(x2[:, i * w:(i + 1) * w], bd) for i in range(x.shape[1] // w)], axis=-1)
    return x * lax.rsqrt(ms + EPS) * g


def _split3(x):
    hi = x.astype(BF16)
    r = x - hi.astype(F32)
    mid = r.astype(BF16)
    lo = (r - mid.astype(F32)).astype(BF16)
    return hi, mid, lo


def _pack3(x, stride):
    hi, mid, lo = (p.astype(F32) for p in _split3(x))
    return hi + pltpu.roll(mid, stride, axis=1) + pltpu.roll(lo, 2 * stride, axis=1)


def _const_spec(shape):
    nd = len(shape)
    return pl.BlockSpec(shape, lambda *_: (0,) * nd, pipeline_mode=pl.Buffered(1))


def _params(n_axes):
    return pltpu.CompilerParams(dimension_semantics=("arbitrary",) * n_axes,
                                vmem_limit_bytes=VMEM_LIMIT)


def _ffn_chunk(d_ff):
    return MXU_TILE if d_ff % MXU_TILE == 0 else LANES


def _ffn_kernel(x_ref, g_ref, wgu_ref, wd_ref, o_ref, act_ref, *, d_ff, chunk):
    x = x_ref[...]
    h = _rms(x, g_ref[...]).astype(BF16)
    for c in range(d_ff // chunk):
        a = _dot(h, wgu_ref[:, c * chunk:(c + 1) * chunk])
        b = _dot(h, wgu_ref[:, d_ff + c * chunk:d_ff + (c + 1) * chunk])
        act_ref[:, c * chunk:(c + 1) * chunk] = (jax.nn.silu(a) * b).astype(BF16)
    o_ref[...] = x + 0.5 * _dot(act_ref[...], wd_ref[...])


def _ffn_call(x, g, wgu, wd, tm):
    n, d = x.shape
    d_ff = wd.shape[0]
    chunk = _ffn_chunk(d_ff)
    return pl.pallas_call(
        functools.partial(_ffn_kernel, d_ff=d_ff, chunk=chunk),
        out_shape=jax.ShapeDtypeStruct((n, d), F32),
        grid=(n // tm,),
        in_specs=[pl.BlockSpec((tm, d), lambda i: (i, 0)),
                  _const_spec(g.shape), _const_spec(wgu.shape), _const_spec(wd.shape)],
        out_specs=pl.BlockSpec((tm, d), lambda i: (i, 0)),
        scratch_shapes=[pltpu.VMEM((tm, d_ff), BF16)],
        compiler_params=_params(1),
        name="ffn",
    )(x, g, wgu, wd)


def _mixer_common(x_ref, g_ref, wqkv_ref, wf_ref, bf_ref, wugv_ref, qg_ref, kg_ref, gvg_ref, bd_ref):
    h = _rms(x_ref[...], g_ref[...]).astype(BF16)
    aw = qg_ref.shape[1]
    qkv = _dot(h, wqkv_ref[...])
    q = _group_rms(qkv[:, :aw], qg_ref[...], bd_ref[...])
    k = _group_rms(qkv[:, aw:2 * aw], kg_ref[...], bd_ref[...])
    v = qkv[:, 2 * aw:]
    logf = jax.nn.log_sigmoid(_dot(h, wf_ref[...]) + bf_ref[...])
    ugv = _dot(h, wugv_ref[...])
    gw = gvg_ref.shape[1]
    u = jax.nn.gelu(ugv[:, :gw])
    gv = _group_rms(jax.nn.gelu(ugv[:, gw:]), gvg_ref[...], bd_ref[...])
    return q, k, v, logf, u, gv


def _mixer_prompt_kernel(x_ref, g_ref, wqkv_ref, wf_ref, bf_ref, wugv_ref, qg_ref, kg_ref, gvg_ref,
                         bd_ref, tri_ref, selqt_ref, selk_ref, ws_ref, bs_ref,
                         qt_ref, qat_ref, k_ref, kp_ref, ka_ref, v_ref, vt_ref, lf_ref, mg_ref,
                         carry_ref, *, tiles_per_seq, n_heads, chunk):
    q, k, v, logf, u, gv = _mixer_common(x_ref, g_ref, wqkv_ref, wf_ref, bf_ref, wugv_ref,
                                         qg_ref, kg_ref, gvg_ref, bd_ref)
    tm = q.shape[0]
    qt_ref[0] = q.T.astype(BF16)
    k_ref[0] = k.T
    kp_ref[...] = k.astype(BF16)
    v_t = v.T
    v_ref[0] = v_t
    vt_ref[0] = v_t.astype(BF16)
    lf_ref[0] = logf.T[:n_heads]

    @pl.when(pl.program_id(0) % tiles_per_seq == 0)
    def _():
        carry_ref[...] = jnp.zeros_like(carry_ref)

    lane = lax.broadcasted_iota(jnp.int32, logf.shape, 1)
    lf = jnp.where(lane < n_heads, logf, 0.0)
    cs = _dot(tri_ref[...], _pack3(lf, PIECE_STRIDE).astype(BF16))
    cs = cs + pltpu.roll(cs, LANES - PIECE_STRIDE, axis=1) + pltpu.roll(cs, LANES - 2 * PIECE_STRIDE, axis=1)
    f_cum = jnp.where(lane < n_heads, cs, 0.0) + carry_ref[...]
    carry_ref[...] = f_cum[tm - 1:tm, :]
    f_ext = _pack3(jnp.where(lane == n_heads, 1.0, f_cum * LOG2E), PIECE_STRIDE)
    ka_ref[...] = _dot(f_ext.astype(BF16), selk_ref[...]).astype(BF16)
    qat_ref[0] = _dot(selqt_ref[...], f_ext.T.astype(BF16)).astype(BF16)

    gvb = gv.astype(BF16)
    r_i = lax.broadcasted_iota(jnp.int32, (chunk, chunk), 0)
    c_i = lax.broadcasted_iota(jnp.int32, (chunk, chunk), 1)
    lane_c = lax.broadcasted_iota(jnp.int32, (chunk, tm // chunk * LANES), 1) & (LANES - 1)
    n_groups = ws_ref.shape[0]
    gdim = gvb.shape[1] // n_groups
    per_blk = LANES // gdim
    ws = [jnp.where(c_i <= r_i, ws_ref[g], jnp.zeros((), BF16)) for g in range(n_groups)]
    n_c = tm // chunk
    for jb in range(gvb.shape[1] // LANES):
        cols = slice(jb * LANES, (jb + 1) * LANES)
        blk = jnp.concatenate([gvb[c * chunk:(c + 1) * chunk, cols] for c in range(n_c)], axis=1)
        sp = _dot(ws[jb * per_blk], blk)
        for gi in range(1, per_blk):
            sp = jnp.where(lane_c >= gi * gdim, _dot(ws[jb * per_blk + gi], blk), sp)
        for c in range(n_c):
            rows = slice(c * chunk, (c + 1) * chunk)
            mg_ref[rows, cols] = (u[rows, cols] * (sp[:, c * LANES:(c + 1) * LANES] + bs_ref[:, cols])).astype(BF16)


def _mixer_sample_kernel(x_ref, g_ref, wqkv_ref, wf_ref, bf_ref, wugv_ref, qg_ref, kg_ref, gvg_ref,
                         bd_ref, w00_ref, b0_ref,
                         q_ref, k_ref, v_ref, lf_ref, gv_ref, mg_ref):
    q, k, v, logf, u, gv = _mixer_common(x_ref, g_ref, wqkv_ref, wf_ref, bf_ref, wugv_ref,
                                         qg_ref, kg_ref, gvg_ref, bd_ref)
    q_ref[...] = q
    k_ref[...] = k
    v_ref[...] = v
    lf_ref[...] = logf
    gv_ref[...] = gv
    mg_ref[...] = (u * (gv * w00_ref[...] + b0_ref[...])).astype(BF16)


def _attn_kernel(qt_ref, qat_ref, kp_ref, ka_ref, vt_ref, o_ref, m_ref, acc_ref,
                 s_ref, p_ref, al_ref, bm_ref, *, tq, hd):
    n_sub = LANES // hd
    row_q = lax.broadcasted_iota(jnp.int32, (2 * LANES, tq), 0) & (LANES - 1)
    head_rows = [(row_q >= h * hd) & (row_q < (h + 1) * hd) for h in range(n_sub)]

    def query_block(i, carry):
        q0 = pl.multiple_of(i * tq, tq)
        qcat = jnp.concatenate([qt_ref[0, :, pl.ds(q0, tq)], qat_ref[0, :, pl.ds(q0, tq)]], axis=0)
        qs = [jnp.where(head_rows[h], qcat, jnp.zeros((), BF16)) for h in range(n_sub)]
        m_ref[...] = jnp.full_like(m_ref, NEG_INF)
        acc_ref[...] = jnp.zeros_like(acc_ref)
        p_ref[1] = jnp.zeros_like(p_ref[1])
        al_ref[1] = jnp.ones_like(al_ref[1])

        def scores(t, slot):
            start = pl.multiple_of(t * tq, tq)
            kc = jnp.concatenate([kp_ref[0, pl.ds(start, tq), :], ka_ref[0, pl.ds(start, tq), :]], axis=-1)
            for h in range(n_sub):
                s = _dot(kc, qs[h])
                s_ref[slot, h] = s
                bm_ref[slot, h] = jnp.max(s, axis=0, keepdims=True)

        def softmax(slot, masked):
            for h in range(n_sub):
                s = s_ref[slot, h]
                if masked:
                    key = lax.broadcasted_iota(jnp.int32, s.shape, 0)
                    qry = lax.broadcasted_iota(jnp.int32, s.shape, 1)
                    s = jnp.where(key <= qry, s, NEG_INF)
                    blk_max = jnp.max(s, axis=0, keepdims=True)
                else:
                    blk_max = bm_ref[slot, h]
                m_prev = m_ref[h]
                m_next = jnp.maximum(m_prev, blk_max)
                alpha = jnp.exp2(m_prev - m_next)
                p = jnp.exp2(s - m_next)
                m_ref[h] = m_next
                al_ref[slot, h] = alpha
                p_ref[slot, h] = p.astype(BF16)

        def values(t, slot):
            start = pl.multiple_of(jnp.maximum(t, 0) * tq, tq)
            for h in range(n_sub):
                v_t = jnp.concatenate([vt_ref[0, h * hd:(h + 1) * hd, pl.ds(start, tq)],
                                       jnp.ones((BF16_ROWS, tq), BF16)], axis=0)
                acc_ref[h] = al_ref[slot, h] * acc_ref[h] + _dot(v_t, p_ref[slot, h])

        UNROLL = 2

        def stage(t, slot):
            scores(t + 1, 1 - slot)
            softmax(slot, False)
            values(t - 1, 1 - slot)

        def body(u, c):
            for k in range(UNROLL):
                stage(UNROLL * u + k, k % 2)
            return c

        scores(0, 0)
        lax.fori_loop(0, i // UNROLL, body, 0)

        for rem in range(UNROLL):
            @pl.when(i % UNROLL == rem)
            def _(rem=rem):
                for k in range(rem):
                    stage(i - rem + k, k % 2)
                softmax(rem % 2, True)
                values(i - 1, 1 - rem % 2)
                values(i, rem % 2)

        out_t = jnp.concatenate([acc_ref[h, :hd] / acc_ref[h, hd:hd + 1] for h in range(n_sub)], axis=0)
        o_ref[0, pl.ds(q0, tq), :] = out_t.T.astype(o_ref.dtype)
        return carry

    lax.fori_loop(0, o_ref.shape[1] // tq, query_block, 0)


def _attn_call(qt, qat, kp, ka, vt, tq, hd):
    b, s, w = kp.shape
    n_sub = LANES // hd
    t_spec = pl.BlockSpec((1, LANES, s), lambda bi, hp: (bi, hp, 0))
    k_spec = pl.BlockSpec((1, s, LANES), lambda bi, hp: (bi, 0, hp))
    return pl.pallas_call(
        functools.partial(_attn_kernel, tq=tq, hd=hd),
        out_shape=jax.ShapeDtypeStruct((b, s, w), BF16),
        grid=(b, w // LANES),
        in_specs=[t_spec, t_spec, k_spec, k_spec, t_spec],
        out_specs=k_spec,
        scratch_shapes=[pltpu.VMEM((n_sub, 1, tq), F32),
                        pltpu.VMEM((n_sub, hd + BF16_ROWS, tq), F32), pltpu.VMEM((2, n_sub, tq, tq), F32),
                        pltpu.VMEM((2, n_sub, tq, tq), BF16), pltpu.VMEM((2, n_sub, 1, tq), F32),
                        pltpu.VMEM((2, n_sub, 1, tq), F32)],
        compiler_params=_params(2),
        name="fox_attn_prompt",
    )(qt, qat, kp, ka, vt)


def _decode_init(fn_ref, m_ref, l_ref, acc_ref, car_ref):
    m_ref[...] = jnp.full_like(m_ref, NEG_INF)
    l_ref[...] = jnp.zeros_like(l_ref)
    acc_ref[...] = jnp.zeros_like(acc_ref)
    car_ref[...] = jnp.broadcast_to(fn_ref[0], car_ref.shape)


def _order_token(x):
    bits = pltpu.bitcast(x, jnp.uint32)
    return pltpu.bitcast(lax.shift_right_logical(bits, jnp.uint32(32)), F32)


def _decode_scores(k_pages, lf_pages, qcol_ref, ltri_ref, m_ref, l_ref, car_ref, w_ref, *, n_heads, hd):
    g_pages = len(k_pages)
    page = ltri_ref.shape[0]
    lf = jnp.concatenate([r[...] for r in lf_pages], axis=0)
    both = _dot(jnp.concatenate(_split3(lf), axis=0), ltri_ref[...])
    gh = g_pages * n_heads
    both = both[:gh] + both[gh:2 * gh] + both[2 * gh:]
    inner = both[:, :page]
    total = both[:, page:]
    carry = car_ref[...]
    bias = [None] * g_pages
    for r in reversed(range(g_pages)):
        bias[r] = inner[r * n_heads:(r + 1) * n_heads] + carry
        carry = carry + total[r * n_heads:(r + 1) * n_heads]
    car_ref[...] = carry

    s_rows = [[None] * n_heads for _ in range(g_pages)]
    for h in range(n_heads):
        hs = slice(h * hd, (h + 1) * hd)
        qh = qcol_ref[0, hs, :]
        for r in range(g_pages):
            s_rows[r][h] = jnp.sum(k_pages[r][hs, :] * qh, axis=0, keepdims=True)
    s = [jnp.concatenate(s_rows[r], axis=0) + bias[r] * LOG2E for r in range(g_pages)]
    m_prev = m_ref[...]
    m_next = m_prev
    for r in range(g_pages):
        m_next = jnp.maximum(m_next, s[r])
    m_next = jnp.broadcast_to(jnp.max(m_next, axis=-1, keepdims=True), m_prev.shape)
    alpha = jnp.exp2(m_prev - m_next)
    p = [jnp.exp2(s[r] - m_next) for r in range(g_pages)]
    l_ref[...] = alpha * l_ref[...] + sum(p[1:], p[0])
    m_ref[...] = m_next
    for r in range(g_pages):
        w_ref[r] = p[r]
    w_ref[g_pages] = alpha
    return _order_token(alpha)


def _decode_values(v_pages, w_ref, acc_ref, token, *, n_heads, hd):
    g_pages = len(v_pages)
    rescale = w_ref[g_pages] + token
    for h in range(n_heads):
        hs = slice(h * hd, (h + 1) * hd)
        pv = v_pages[0][hs, :] * w_ref[0, h:h + 1, :]
        for r in range(1, g_pages):
            pv = pv + v_pages[r][hs, :] * w_ref[r, h:h + 1, :]
        acc_ref[hs, :] = rescale[h:h + 1, :] * acc_ref[hs, :] + pv
    return _order_token(acc_ref[0:8, :])


def _decode_finish(q_ref, kn_ref, vn_ref, m_ref, l_ref, acc_ref, o_ref, *, n_heads, hd):
    width = n_heads * hd
    row = lax.broadcasted_iota(jnp.int32, (n_heads, width), 0)
    col = lax.broadcasted_iota(jnp.int32, (n_heads, width), 1)
    head_mask = (col >= row * hd) & (col < (row + 1) * hd)

    def head_row(x):
        return jnp.sum(jnp.where(head_mask, jnp.broadcast_to(x, (n_heads, width)), 0.0),
                       axis=0, keepdims=True)

    qbd = jnp.where(head_mask, jnp.broadcast_to(q_ref[0], (n_heads, width)), 0.0)
    s_new = jnp.sum(qbd * kn_ref[0], axis=-1, keepdims=True)
    m_old = m_ref[:, :1]
    m_fin = jnp.maximum(m_old, s_new)
    a_fin = jnp.exp2(m_old - m_fin)
    p_new = jnp.exp2(s_new - m_fin)
    l_fin = a_fin * jnp.sum(l_ref[...], axis=-1, keepdims=True) + p_new
    past = jnp.sum(acc_ref[...].T, axis=0, keepdims=True)
    out = (head_row(a_fin) * past + head_row(p_new) * vn_ref[0]) / head_row(l_fin)
    o_ref[0] = out.astype(o_ref.dtype)


def _ffn_decode_kernel(pt_ref, x_ref, g_ref, wgu_ref, wd_ref, qcol_ref, q_ref, kn_ref, vn_ref, fn_ref,
                       ltri_ref, ck_hbm, cv_hbm, clf_hbm, *refs,
                       d_ff, chunk, down_block, b0, g_pages, n_chunks, steps_per_sample, n_heads, hd, with_ple):
    if with_ple:
        pe_ref, pg_ref, wpg_ref, wpp_ref = refs[:4]
        refs = refs[4:]
    o_ref, a_ref, act_ref, kbuf, vbuf, lfbuf, sem, m_ref, l_ref, acc_ref, car_ref, w_ref = refs
    step = pl.program_id(0)
    n_steps = pl.num_programs(0)
    n_groups = steps_per_sample * n_chunks

    def page_ids(st, c):
        b = b0 + st // steps_per_sample
        first = (n_groups - 1 - ((st % steps_per_sample) * n_chunks + c)) * g_pages
        return [pt_ref[b, first + r] for r in range(g_pages)]

    def k_copies(st, c):
        slot = c % n_slots
        out = []
        for r, pid in enumerate(page_ids(st, c)):
            out.append(pltpu.make_async_copy(ck_hbm.at[pid], kbuf.at[slot, r], sem.at[slot, 0]))
            out.append(pltpu.make_async_copy(clf_hbm.at[pid], lfbuf.at[slot, r], sem.at[slot, 2]))
        return out

    def v_copies(st, c):
        slot = c % n_slots
        return [pltpu.make_async_copy(cv_hbm.at[pid], vbuf.at[slot, r], sem.at[slot, 1])
                for r, pid in enumerate(page_ids(st, c))]

    n_slots = kbuf.shape[0]
    k_ahead, v_ahead = n_slots - 1, n_slots - 2

    def start_ahead(copies_of, c):
        if c < n_chunks:
            for cp in copies_of(step, c):
                cp.start()
        else:
            @pl.when(step + 1 < n_steps)
            def _():
                for cp in copies_of(step + 1, c - n_chunks):
                    cp.start()

    @pl.when(step == 0)
    def _():
        for c in range(k_ahead):
            for cp in k_copies(step, c):
                cp.start()
        for c in range(v_ahead):
            for cp in v_copies(step, c):
                cp.start()

    @pl.when(step % steps_per_sample == 0)
    def _():
        _decode_init(fn_ref, m_ref, l_ref, acc_ref, car_ref)

    def values(c, token):
        return _decode_values([vbuf.at[c % n_slots, r] for r in range(g_pages)], w_ref.at[c % 2], acc_ref,
                              token, n_heads=n_heads, hd=hd)

    def decode(c):
        slot = c % n_slots
        start_ahead(k_copies, c + k_ahead)
        start_ahead(v_copies, c + v_ahead)
        for cp in k_copies(step, c):
            cp.wait()
        if c > 0:
            for cp in v_copies(step, c - 1):
                cp.wait()
        token = _decode_scores([kbuf.at[slot, r] for r in range(g_pages)],
                               [lfbuf.at[slot, r] for r in range(g_pages)],
                               qcol_ref, ltri_ref, m_ref, l_ref, car_ref, w_ref.at[c % 2],
                               n_heads=n_heads, hd=hd)
        return values(c - 1, token) if c > 0 else token

    x = x_ref[...]
    h = _rms(x, g_ref[...]).astype(BF16)

    def waits_for(token, val):
        if token is None:
            return val
        return val + jnp.tile(token, (val.shape[0] // token.shape[0], val.shape[1] // token.shape[1]))

    def up(c, token=None):
        a = _dot(h, wgu_ref[:, c * chunk:(c + 1) * chunk])
        b = _dot(h, wgu_ref[:, d_ff + c * chunk:d_ff + (c + 1) * chunk])
        act_ref[:, c * chunk:(c + 1) * chunk] = (jax.nn.silu(a) * waits_for(token, b)).astype(BF16)

    def down(nb, token=None):
        cols = slice(nb * down_block, (nb + 1) * down_block)
        o_ref[:, cols] = waits_for(token, x[:, cols]) + 0.5 * _dot(act_ref[...], wd_ref[:, cols])

    def ple(token=None):
        x3 = waits_for(token, o_ref[...])
        gate = jax.nn.sigmoid(_dot(_rms(x3, pg_ref[...]).astype(BF16), wpg_ref[...]))
        o_ref[...] = x3 + gate * _dot(pe_ref[...].astype(BF16), wpp_ref[...])

    units = ([functools.partial(up, c) for c in range(d_ff // chunk)]
             + [functools.partial(down, nb) for nb in range(x.shape[1] // down_block)]
             + ([ple] if with_ple else []))
    for u, unit in enumerate(units):
        token = None
        for c in range(n_chunks):
            if (c * len(units)) // n_chunks == u:
                token = decode(c)
        unit(token) if token is not None else unit()
    for cp in v_copies(step, n_chunks - 1):
        cp.wait()
    values(n_chunks - 1, _order_token(w_ref[(n_chunks - 1) % 2, g_pages]))

    @pl.when(step % steps_per_sample == steps_per_sample - 1)
    def _():
        _decode_finish(q_ref, kn_ref, vn_ref, m_ref, l_ref, acc_ref, a_ref, n_heads=n_heads, hd=hd)


def _ffn_decode_call(x, g, wgu, wd, tm, page_table, b0, n_samples, qcol, q, kn, vn, fn,
                     cache_kt, cache_vt, cache_lf_t, g_pages, n_heads, hd, ple=None):
    n, d = x.shape
    d_ff = wd.shape[0]
    chunk = _ffn_chunk(d_ff)
    n_steps = n // tm
    n_pages = page_table.shape[1]
    n_pool, width, page = cache_kt.shape
    assert n_steps % n_samples == 0
    steps_per_sample = n_steps // n_samples
    assert n_pages % (steps_per_sample * g_pages) == 0
    n_chunks = n_pages // (steps_per_sample * g_pages)
    down_block = 2 * MXU_TILE if d % (2 * MXU_TILE) == 0 else d
    n_slots = CACHE_RING_SLOTS
    assert n_chunks % n_slots == 0, "a chunk's ring slot must not depend on the grid step"
    ltri = jnp.asarray(np.concatenate([np.tril(np.ones((page, page), np.float32), -1),
                                       np.ones((page, page), np.float32)], axis=1), BF16)
    sample = lambda i, pt: (b0 + i // steps_per_sample, 0, 0)
    const = lambda shape: pl.BlockSpec(shape, lambda i, pt: (0,) * len(shape), pipeline_mode=pl.Buffered(1))
    row_spec = pl.BlockSpec((1, 1, width), sample)
    hbm = pl.BlockSpec(memory_space=pl.ANY)
    ple_specs, ple_args = [], ()
    if ple is not None:
        ple_args = tuple(ple)
        ple_specs = ([pl.BlockSpec((tm, ple[0].shape[1]), lambda i, pt: (i, 0))]
                     + [const(t.shape) for t in ple[1:]])
    return pl.pallas_call(
        functools.partial(_ffn_decode_kernel, d_ff=d_ff, chunk=chunk, down_block=down_block, b0=b0, g_pages=g_pages,
                          n_chunks=n_chunks, steps_per_sample=steps_per_sample, n_heads=n_heads, hd=hd,
                          with_ple=ple is not None),
        out_shape=(jax.ShapeDtypeStruct((n, d), F32), jax.ShapeDtypeStruct((n_samples, 1, width), BF16)),
        grid_spec=pltpu.PrefetchScalarGridSpec(
            num_scalar_prefetch=1,
            grid=(n_steps,),
            in_specs=[pl.BlockSpec((tm, d), lambda i, pt: (i, 0)),
                      const(g.shape), const(wgu.shape), const(wd.shape),
                      pl.BlockSpec((1, width, page), sample), row_spec, row_spec, row_spec,
                      pl.BlockSpec((1, n_heads, page), sample), const(ltri.shape), hbm, hbm, hbm] + ple_specs,
            out_specs=(pl.BlockSpec((tm, d), lambda i, pt: (i, 0)),
                       pl.BlockSpec((1, 1, width), lambda i, pt: (i // steps_per_sample, 0, 0))),
            scratch_shapes=[pltpu.VMEM((tm, d_ff), BF16),
                            pltpu.VMEM((n_slots, g_pages, width, page), F32),
                            pltpu.VMEM((n_slots, g_pages, width, page), F32),
                            pltpu.VMEM((n_slots, g_pages, n_heads, page), F32),
                            pltpu.SemaphoreType.DMA((n_slots, 3)),
                            pltpu.VMEM((n_heads, page), F32), pltpu.VMEM((n_heads, page), F32),
                            pltpu.VMEM((width, page), F32), pltpu.VMEM((n_heads, page), F32),
                            pltpu.VMEM((2, g_pages + 1, n_heads, page), F32)]),
        compiler_params=_params(1),
        name="ffn_decode",
    )(page_table, x, g, wgu, wd, qcol, q, kn, vn, fn, ltri, cache_kt, cache_vt, cache_lf_t, *ple_args)


def _merge_kernel(x_ref, a_ref, mg_ref, g_ref, wga_ref, wgb_ref, wpa_ref, wpg_ref, wo_ref, o_ref):
    x = x_ref[...]
    h = _rms(x, g_ref[...]).astype(BF16)
    ga = jax.nn.sigmoid(_dot(h, wga_ref[...]))
    gb = jax.nn.sigmoid(_dot(h, wgb_ref[...]))
    merged = ga * _dot(a_ref[...], wpa_ref[...]) + gb * _dot(mg_ref[...], wpg_ref[...])
    o_ref[...] = x + _dot(merged.astype(BF16), wo_ref[...])


def _merge_call(x, a, mg, g, wga, wgb, wpa, wpg, wo, tm):
    n, d = x.shape
    w = a.shape[1]
    row = lambda cols: pl.BlockSpec((tm, cols), lambda i: (i, 0))
    return pl.pallas_call(
        _merge_kernel,
        out_shape=jax.ShapeDtypeStruct((n, d), F32),
        grid=(n // tm,),
        in_specs=[row(d), row(w), row(mg.shape[1])] + [_const_spec(t.shape) for t in (g, wga, wgb, wpa, wpg, wo)],
        out_specs=row(d),
        compiler_params=_params(1),
        name="merge_out",
    )(x, a, mg, g, wga, wgb, wpa, wpg, wo)


def _ple_kernel(x_ref, p_ref, g_ref, wg_ref, wp_ref, o_ref):
    x = x_ref[...]
    gate = jax.nn.sigmoid(_dot(_rms(x, g_ref[...]).astype(BF16), wg_ref[...]))
    o_ref[...] = x + gate * _dot(p_ref[...].astype(BF16), wp_ref[...])


def _ple_call(x, p, g, wg, wp, tm):
    n, d = x.shape
    row = lambda cols: pl.BlockSpec((tm, cols), lambda i: (i, 0))
    return pl.pallas_call(
        _ple_kernel,
        out_shape=jax.ShapeDtypeStruct((n, d), F32),
        grid=(n // tm,),
        in_specs=[row(d), row(p.shape[1])] + [_const_spec(t.shape) for t in (g, wg, wp)],
        out_specs=row(d),
        compiler_params=_params(1),
        name="ple",
    )(x, p, g, wg, wp)


class _Tiles(NamedTuple):
    rows: int
    merge_rows: int
    attn_block: int
    cache_pages: int


def _tiles(n_tokens, seq):
    rows = 512
    merge_rows = 1024 if n_tokens % 1024 == 0 else rows
    attn_block = 512
    assert n_tokens % rows == 0 and seq % rows == 0 and seq % attn_block == 0
    return _Tiles(rows=rows, merge_rows=merge_rows, attn_block=attn_block, cache_pages=8)


def _block_diag_mean(width, group):
    idx = np.arange(width) // group
    return jnp.asarray((idx[:, None] == idx[None, :]).astype(np.float32) / group, BF16)


def _forget_selectors(n_heads, hd):
    width = n_heads * hd
    selq = np.zeros((LANES, width), np.float32)
    selk = np.zeros((LANES, width), np.float32)
    for h in range(n_heads):
        for p in range(3):
            selq[p * PIECE_STRIDE + h, h * hd + p] = 1.0
            selk[p * PIECE_STRIDE + h, h * hd + 3 + p] = -1.0
            selq[n_heads, h * hd + 3 + p] = 1.0
            selk[n_heads, h * hd + p] = 1.0
    return jnp.asarray(selq.T, BF16), jnp.asarray(selk, BF16)


def kernel(x_prompt, x_sample, cache_k, cache_v, cache_logf, page_table, p_prompt, p_sample,
           ffn1_norm, ffn1_w_gu, ffn1_w_down, mix_norm, w_in, b_forget, q_norm, k_norm,
           gmlp_v_norm, w_spatial, b_spatial, w_proj_attn, w_proj_gmlp, w_out,
           ffn2_norm, ffn2_w_gu, ffn2_w_down, ple_norm, ple_w_gate, ple_w_proj):
    depth = ffn1_norm.shape[0]
    bsz, seq, d = x_prompt.shape
    db = x_sample.shape[0]
    assert x_sample.shape[1] == 1, "the sample group decodes one token per step"
    n_heads, hd = cache_k.shape[3], cache_k.shape[4]
    aw = n_heads * hd
    n_groups, chunk = w_spatial.shape[1], w_spatial.shape[2]
    gw = w_proj_gmlp.shape[1]
    gdim = gw // n_groups
    page = cache_k.shape[2]
    n_pool = cache_k.shape[1]
    assert hd == gdim and aw == gw and LANES % hd == 0 and n_heads < PIECE_STRIDE and page == LANES

    tiles = _tiles(bsz * seq, seq)
    tm, tq, g_pages = tiles.rows, tiles.attn_block, tiles.cache_pages
    scale = hd ** -0.5
    tri = jnp.asarray(np.tril(np.ones((tm, tm), np.float32)), BF16)
    bd = _block_diag_mean(math.gcd(aw, MXU_TILE), hd)
    selqt, selk = _forget_selectors(n_heads, hd)

    xp = x_prompt.reshape(bsz * seq, d)
    xs = x_sample.reshape(db, d)
    outs = {k: [] for k in ("kp", "vp", "fp", "ks", "vs", "fs", "gs")}
    for li in range(depth):
        bf = lambda t: t[li].astype(BF16)
        row = lambda t: t[li].reshape(1, -1).astype(F32)
        tile_h = lambda t, mult=1.0: jnp.tile(t[li].astype(F32) * mult, n_heads).reshape(1, -1)
        w1gu, w1d, w2gu, w2d = bf(ffn1_w_gu), bf(ffn1_w_down), bf(ffn2_w_gu), bf(ffn2_w_down)
        wi = w_in[li]
        o = 0
        wqkv = wi[:, o:o + 3 * aw].astype(BF16); o += 3 * aw
        wf = jnp.pad(wi[:, o:o + n_heads], ((0, 0), (0, LANES - n_heads))).astype(BF16); o += n_heads
        wugv = wi[:, o:o + 2 * gw].astype(BF16); o += 2 * gw
        wga = wi[:, o:o + d].astype(BF16); o += d
        wgb = wi[:, o:o + d].astype(BF16)
        bfg = jnp.pad(b_forget[li].astype(F32), (0, LANES - n_heads)).reshape(1, LANES)
        qg, kg, gvg = tile_h(q_norm, scale * LOG2E), tile_h(k_norm), tile_h(gmlp_v_norm)
        ws = bf(w_spatial)
        bs = jnp.repeat(b_spatial[li].astype(F32).T, gdim, axis=1)
        w00 = jnp.repeat(w_spatial[li, :, 0, 0].astype(F32), gdim).reshape(1, gw)
        b0 = jnp.repeat(b_spatial[li, :, 0].astype(F32), gdim).reshape(1, gw)
        wpa, wpg, wo = bf(w_proj_attn), bf(w_proj_gmlp), bf(w_out)
        plg, plp = bf(ple_w_gate), bf(ple_w_proj)
        mixer_w = (row(mix_norm), wqkv, wf, bfg, wugv, qg, kg, gvg, bd)

        x1s = _ffn_call(xs, row(ffn1_norm), w1gu, w1d, db)
        sconsts = mixer_w + (w00, b0)
        full = lambda cols: pl.BlockSpec((db, cols), lambda i: (0, 0))
        ssd = lambda cols, dt: jax.ShapeDtypeStruct((db, cols), dt)
        qs, ksn, vsn, lfs, gvs, mgs = pl.pallas_call(
            _mixer_sample_kernel,
            out_shape=(ssd(aw, F32), ssd(aw, F32), ssd(aw, F32), ssd(LANES, F32), ssd(gw, F32), ssd(gw, BF16)),
            grid=(1,),
            in_specs=[full(d)] + [_const_spec(t.shape) for t in sconsts],
            out_specs=tuple(full(c) for c in (aw, aw, aw, LANES, gw, gw)),
            compiler_params=_params(1),
            name="mixer_sample",
        )(x1s, *sconsts)
        fn = jnp.broadcast_to(lfs[:, :n_heads, None], (db, n_heads, page))
        qcol = jnp.broadcast_to(qs[:, :, None], (db, aw, page))
        cache_lf_t = jnp.swapaxes(cache_logf[li].astype(F32), 1, 2)
        cache_kt = jnp.transpose(cache_k[li], (0, 2, 3, 1)).reshape(n_pool, aw, page)
        cache_vt = jnp.transpose(cache_v[li], (0, 2, 3, 1)).reshape(n_pool, aw, page)
        decode_args = (qcol, qs.reshape(db, 1, aw), ksn.reshape(db, 1, aw), vsn.reshape(db, 1, aw), fn,
                       cache_kt, cache_vt, cache_lf_t, g_pages, n_heads, hd)
        half = db // 2

        n = bsz * seq
        x1, a_s0 = _ffn_decode_call(xp, row(ffn1_norm), w1gu, w1d, tm, page_table, 0, half, *decode_args)
        rowspec = lambda cols: pl.BlockSpec((tm, cols), lambda i: (i, 0))
        consts = mixer_w + (tri, selqt, selk, ws, bs)
        sds = lambda cols, dt: jax.ShapeDtypeStruct((n, cols), dt)
        tps = seq // tm
        tsd = lambda rows_, dt: jax.ShapeDtypeStruct((bsz, rows_, seq), dt)
        tspec = lambda rows_: pl.BlockSpec((1, rows_, tm), lambda i: (i // tps, 0, i % tps))
        qt, qat, k32t, kp, ka, v32t, vt, lfpt, mg = pl.pallas_call(
            functools.partial(_mixer_prompt_kernel, tiles_per_seq=tps, n_heads=n_heads, chunk=chunk),
            out_shape=(tsd(aw, BF16), tsd(aw, BF16), tsd(aw, F32), sds(aw, BF16), sds(aw, BF16),
                       tsd(aw, F32), tsd(aw, BF16), tsd(n_heads, F32), sds(gw, BF16)),
            grid=(n // tm,),
            in_specs=[rowspec(d)] + [_const_spec(t.shape) for t in consts],
            out_specs=(tspec(aw), tspec(aw), tspec(aw), rowspec(aw), rowspec(aw),
                       tspec(aw), tspec(aw), tspec(n_heads), rowspec(gw)),
            scratch_shapes=[pltpu.VMEM((1, LANES), F32)],
            compiler_params=_params(1),
            name="mixer_prompt",
        )(x1, *consts)
        r3 = lambda t: t.reshape(bsz, seq, aw)
        a = _attn_call(qt, qat, r3(kp), r3(ka), vt, tq, hd).reshape(n, aw)
        x2 = _merge_call(x1, a, mg, row(mix_norm), wga, wgb, wpa, wpg, wo, tiles.merge_rows)
        xp, a_s1 = _ffn_decode_call(x2, row(ffn2_norm), w2gu, w2d, tm, page_table, half, db - half,
                                    *decode_args,
                                    ple=(p_prompt[li].reshape(n, -1), row(ple_norm), plg, plp))
        outs["kp"].append(k32t.reshape(bsz, n_heads, hd, seq).transpose(0, 3, 1, 2))
        outs["vp"].append(v32t.reshape(bsz, n_heads, hd, seq).transpose(0, 3, 1, 2))
        outs["fp"].append(lfpt.transpose(0, 2, 1))

        a_s = jnp.concatenate([a_s0, a_s1], axis=0).reshape(db, aw)
        x2s = _merge_call(x1s, a_s, mgs, row(mix_norm), wga, wgb, wpa, wpg, wo, db)
        x3s = _ffn_call(x2s, row(ffn2_norm), w2gu, w2d, db)
        xs = _ple_call(x3s, p_sample[li].reshape(db, -1), row(ple_norm), plg, plp, db)
        outs["ks"].append(ksn.reshape(db, 1, n_heads, hd))
        outs["vs"].append(vsn.reshape(db, 1, n_heads, hd))
        outs["fs"].append(lfs[:, :n_heads].reshape(db, 1, n_heads))
        outs["gs"].append(gvs.reshape(db, 1, n_groups, gdim))

    st = lambda key: jnp.stack(outs[key])
    return (xp.reshape(bsz, seq, d), xs.reshape(db, 1, d), st("kp"), st("vp"), st("fp"),
            st("ks"), st("vs"), st("fs"), st("gs"))
```

```python
import functools
import math
from typing import NamedTuple

import jax
import jax.numpy as jnp
import numpy as np
from jax import lax
from jax.experimental import pallas as pl
from jax.experimental.pallas import tpu as pltpu

EPS = 1e-6
NEG_INF = -1e30
LOG2E = 1.4426950408889634
LANES = 128
BF16_ROWS = 16
MXU_TILE = 256
PIECE_STRIDE = 16
CACHE_RING_SLOTS = 4
F32 = jnp.float32
BF16 = jnp.bfloat16
VMEM_LIMIT = 56 * 1024 * 1024


def _dot(a, b):
    return jnp.dot(a, b, preferred_element_type=F32)


def _rms(x, g):
    ms = jnp.mean(x * x, axis=-1, keepdims=True)
    return x * lax.rsqrt(ms + EPS) * g


def _group_rms(x, g, bd):
    x2 = (x * x).astype(BF16)
    w = bd.shape[0]
    ms = jnp.concatenate([_dot(x2[:, i * w:(i + 1) * w], bd) for i in range(x.shape[1] // w)], axis=-1)
    return x * lax.rsqrt(ms + EPS) * g


def _split3(x):
    hi = x.astype(BF16)
    r = x - hi.astype(F32)
    mid = r.astype(BF16)
    lo = (r - mid.astype(F32)).astype(BF16)
    return hi, mid, lo


def _pack3(x, stride):
    hi, mid, lo = (p.astype(F32) for p in _split3(x))
    return hi + pltpu.roll(mid, stride, axis=1) + pltpu.roll(lo, 2 * stride, axis=1)


def _const_spec(shape):
    nd = len(shape)
    return pl.BlockSpec(shape, lambda *_: (0,) * nd, pipeline_mode=pl.Buffered(1))


def _params(n_axes):
    return pltpu.CompilerParams(dimension_semantics=("arbitrary",) * n_axes,
                                vmem_limit_bytes=VMEM_LIMIT)


def _ffn_chunk(d_ff):
    return MXU_TILE if d_ff % MXU_TILE == 0 else LANES


def _ffn_kernel(x_ref, g_ref, wgu_ref, wd_ref, o_ref, act_ref, *, d_ff, chunk):
    x = x_ref[...]
    h = _rms(x, g_ref[...]).astype(BF16)
    for c in range(d_ff // chunk):
        a = _dot(h, wgu_ref[:, c * chunk:(c + 1) * chunk])
        b = _dot(h, wgu_ref[:, d_ff + c * chunk:d_ff + (c + 1) * chunk])
        act_ref[:, c * chunk:(c + 1) * chunk] = (jax.nn.silu(a) * b).astype(BF16)
    o_ref[...] = x + 0.5 * _dot(act_ref[...], wd_ref[...])


def _ffn_call(x, g, wgu, wd, tm):
    n, d = x.shape
    d_ff = wd.shape[0]
    chunk = _ffn_chunk(d_ff)
    return pl.pallas_call(
        functools.partial(_ffn_kernel, d_ff=d_ff, chunk=chunk),
        out_shape=jax.ShapeDtypeStruct((n, d), F32),
        grid=(n // tm,),
        in_specs=[pl.BlockSpec((tm, d), lambda i: (i, 0)),
                  _const_spec(g.shape), _const_spec(wgu.shape), _const_spec(wd.shape)],
        out_specs=pl.BlockSpec((tm, d), lambda i: (i, 0)),
        scratch_shapes=[pltpu.VMEM((tm, d_ff), BF16)],
        compiler_params=_params(1),
        name="ffn",
    )(x, g, wgu, wd)


def _mixer_common(x_ref, g_ref, wqkv_ref, wf_ref, bf_ref, wugv_ref, qg_ref, kg_ref, gvg_ref, bd_ref):
    h = _rms(x_ref[...], g_ref[...]).astype(BF16)
    aw = qg_ref.shape[1]
    qkv = _dot(h, wqkv_ref[...])
    q = _group_rms(qkv[:, :aw], qg_ref[...], bd_ref[...])
    k = _group_rms(qkv[:, aw:2 * aw], kg_ref[...], bd_ref[...])
    v = qkv[:, 2 * aw:]
    logf = jax.nn.log_sigmoid(_dot(h, wf_ref[...]) + bf_ref[...])
    ugv = _dot(h, wugv_ref[...])
    gw = gvg_ref.shape[1]
    u = jax.nn.gelu(ugv[:, :gw])
    gv = _group_rms(jax.nn.gelu(ugv[:, gw:]), gvg_ref[...], bd_ref[...])
    return q, k, v, logf, u, gv


def _mixer_prompt_kernel(x_ref, g_ref, wqkv_ref, wf_ref, bf_ref, wugv_ref, qg_ref, kg_ref, gvg_ref,
                         bd_ref, tri_ref, selqt_ref, selk_ref, ws_ref, bs_ref,
                         qt_ref, qat_ref, k_ref, kp_ref, ka_ref, v_ref, vt_ref, lf_ref, mg_ref,
                         carry_ref, *, tiles_per_seq, n_heads, chunk):
    q, k, v, logf, u, gv = _mixer_common(x_ref, g_ref, wqkv_ref, wf_ref, bf_ref, wugv_ref,
                                         qg_ref, kg_ref, gvg_ref, bd_ref)
    tm = q.shape[0]
    qt_ref[0] = q.T.astype(BF16)
    k_ref[0] = k.T
    kp_ref[...] = k.astype(BF16)
    v_t = v.T
    v_ref[0] = v_t
    vt_ref[0] = v_t.astype(BF16)
    lf_ref[0] = logf.T[:n_heads]

    @pl.when(pl.program_id(0) % tiles_per_seq == 0)
    def _():
        carry_ref[...] = jnp.zeros_like(carry_ref)

    lane = lax.broadcasted_iota(jnp.int32, logf.shape, 1)
    lf = jnp.where(lane < n_heads, logf, 0.0)
    cs = _dot(tri_ref[...], _pack3(lf, PIECE_STRIDE).astype(BF16))
    cs = cs + pltpu.roll(cs, LANES - PIECE_STRIDE, axis=1) + pltpu.roll(cs, LANES - 2 * PIECE_STRIDE, axis=1)
    f_cum = jnp.where(lane < n_heads, cs, 0.0) + carry_ref[...]
    carry_ref[...] = f_cum[tm - 1:tm, :]
    f_ext = _pack3(jnp.where(lane == n_heads, 1.0, f_cum * LOG2E), PIECE_STRIDE)
    ka_ref[...] = _dot(f_ext.astype(BF16), selk_ref[...]).astype(BF16)
    qat_ref[0] = _dot(selqt_ref[...], f_ext.T.astype(BF16)).astype(BF16)

    gvb = gv.astype(BF16)
    r_i = lax.broadcasted_iota(jnp.int32, (chunk, chunk), 0)
    c_i = lax.broadcasted_iota(jnp.int32, (chunk, chunk), 1)
    lane_c = lax.broadcasted_iota(jnp.int32, (chunk, tm // chunk * LANES), 1) & (LANES - 1)
    n_groups = ws_ref.shape[0]
    gdim = gvb.shape[1] // n_groups
    per_blk = LANES // gdim
    ws = [jnp.where(c_i <= r_i, ws_ref[g], jnp.zeros((), BF16)) for g in range(n_groups)]
    n_c = tm // chunk
    for jb in range(gvb.shape[1] // LANES):
        cols = slice(jb * LANES, (jb + 1) * LANES)
        blk = jnp.concatenate([gvb[c * chunk:(c + 1) * chunk, cols] for c in range(n_c)], axis=1)
        sp = _dot(ws[jb * per_blk], blk)
        for gi in range(1, per_blk):
            sp = jnp.where(lane_c >= gi * gdim, _dot(ws[jb * per_blk + gi], blk), sp)
        for c in range(n_c):
            rows = slice(c * chunk, (c + 1) * chunk)
            mg_ref[rows, cols] = (u[rows, cols] * (sp[:, c * LANES:(c + 1) * LANES] + bs_ref[:, cols])).astype(BF16)


def _mixer_sample_kernel(x_ref, g_ref, wqkv_ref, wf_ref, bf_ref, wugv_ref, qg_ref, kg_ref, gvg_ref,
                         bd_ref, w00_ref, b0_ref,
                         q_ref, k_ref, v_ref, lf_ref, gv_ref, mg_ref):
    q, k, v, logf, u, gv = _mixer_common(x_ref, g_ref, wqkv_ref, wf_ref, bf_ref, wugv_ref,
                                         qg_ref, kg_ref, gvg_ref, bd_ref)
    q_ref[...] = q
    k_ref[...] = k
    v_ref[...] = v
    lf_ref[...] = logf
    gv_ref[...] = gv
    mg_ref[...] = (u * (gv * w00_ref[...] + b0_ref[...])).astype(BF16)


def _attn_kernel(qt_ref, qat_ref, kp_ref, ka_ref, vt_ref, o_ref, m_ref, acc_ref,
                 s_ref, p_ref, al_ref, bm_ref, *, tq, hd):
    n_sub = LANES // hd
    row_q = lax.broadcasted_iota(jnp.int32, (2 * LANES, tq), 0) & (LANES - 1)
    head_rows = [(row_q >= h * hd) & (row_q < (h + 1) * hd) for h in range(n_sub)]

    def query_block(i, carry):
        q0 = pl.multiple_of(i * tq, tq)
        qcat = jnp.concatenate([qt_ref[0, :, pl.ds(q0, tq)], qat_ref[0, :, pl.ds(q0, tq)]], axis=0)
        qs = [jnp.where(head_rows[h], qcat, jnp.zeros((), BF16)) for h in range(n_sub)]
        m_ref[...] = jnp.full_like(m_ref, NEG_INF)
        acc_ref[...] = jnp.zeros_like(acc_ref)
        p_ref[1] = jnp.zeros_like(p_ref[1])
        al_ref[1] = jnp.ones_like(al_ref[1])

        def scores(t, slot):
            start = pl.multiple_of(t * tq, tq)
            kc = jnp.concatenate([kp_ref[0, pl.ds(start, tq), :], ka_ref[0, pl.ds(start, tq), :]], axis=-1)
            for h in range(n_sub):
                s = _dot(kc, qs[h])
                s_ref[slot, h] = s
                bm_ref[slot, h] = jnp.max(s, axis=0, keepdims=True)

        def softmax(slot, masked):
            for h in range(n_sub):
                s = s_ref[slot, h]
                if masked:
                    key = lax.broadcasted_iota(jnp.int32, s.shape, 0)
                    qry = lax.broadcasted_iota(jnp.int32, s.shape, 1)
                    s = jnp.where(key <= qry, s, NEG_INF)
                    blk_max = jnp.max(s, axis=0, keepdims=True)
                else:
                    blk_max = bm_ref[slot, h]
                m_prev = m_ref[h]
                m_next = jnp.maximum(m_prev, blk_max)
                alpha = jnp.exp2(m_prev - m_next)
                p = jnp.exp2(s - m_next)
                m_ref[h] = m_next
                al_ref[slot, h] = alpha
                p_ref[slot, h] = p.astype(BF16)

        def values(t, slot):
            start = pl.multiple_of(jnp.maximum(t, 0) * tq, tq)
            for h in range(n_sub):
                v_t = jnp.concatenate([vt_ref[0, h * hd:(h + 1) * hd, pl.ds(start, tq)],
                                       jnp.ones((BF16_ROWS, tq), BF16)], axis=0)
                acc_ref[h] = al_ref[slot, h] * acc_ref[h] + _dot(v_t, p_ref[slot, h])

        UNROLL = 2

        def stage(t, slot):
            scores(t + 1, 1 - slot)
            softmax(slot, False)
            values(t - 1, 1 - slot)

        def body(u, c):
            for k in range(UNROLL):
                stage(UNROLL * u + k, k % 2)
            return c

        scores(0, 0)
        lax.fori_loop(0, i // UNROLL, body, 0)

        for rem in range(UNROLL):
            @pl.when(i % UNROLL == rem)
            def _(rem=rem):
                for k in range(rem):
                    stage(i - rem + k, k % 2)
                softmax(rem % 2, True)
                values(i - 1, 1 - rem % 2)
                values(i, rem % 2)

        out_t = jnp.concatenate([acc_ref[h, :hd] / acc_ref[h, hd:hd + 1] for h in range(n_sub)], axis=0)
        o_ref[0, pl.ds(q0, tq), :] = out_t.T.astype(o_ref.dtype)
        return carry

    lax.fori_loop(0, o_ref.shape[1] // tq, query_block, 0)


def _attn_call(qt, qat, kp, ka, vt, tq, hd):
    b, s, w = kp.shape
    n_sub = LANES // hd
    t_spec = pl.BlockSpec((1, LANES, s), lambda bi, hp: (bi, hp, 0))
    k_spec = pl.BlockSpec((1, s, LANES), lambda bi, hp: (bi, 0, hp))
    return pl.pallas_call(
        functools.partial(_attn_kernel, tq=tq, hd=hd),
        out_shape=jax.ShapeDtypeStruct((b, s, w), BF16),
        grid=(b, w // LANES),
        in_specs=[t_spec, t_spec, k_spec, k_spec, t_spec],
        out_specs=k_spec,
        scratch_shapes=[pltpu.VMEM((n_sub, 1, tq), F32),
                        pltpu.VMEM((n_sub, hd + BF16_ROWS, tq), F32), pltpu.VMEM((2, n_sub, tq, tq), F32),
                        pltpu.VMEM((2, n_sub, tq, tq), BF16), pltpu.VMEM((2, n_sub, 1, tq), F32),
                        pltpu.VMEM((2, n_sub, 1, tq), F32)],
        compiler_params=_params(2),
        name="fox_attn_prompt",
    )(qt, qat, kp, ka, vt)


def _decode_init(fn_ref, m_ref, l_ref, acc_ref, car_ref):
    m_ref[...] = jnp.full_like(m_ref, NEG_INF)
    l_ref[...] = jnp.zeros_like(l_ref)
    acc_ref[...] = jnp.zeros_like(acc_ref)
    car_ref[...] = jnp.broadcast_to(fn_ref[0], car_ref.shape)


def _order_token(x):
    bits = pltpu.bitcast(x, jnp.uint32)
    return pltpu.bitcast(lax.shift_right_logical(bits, jnp.uint32(32)), F32)


def _decode_scores(k_pages, lf_pages, qcol_ref, ltri_ref, m_ref, l_ref, car_ref, w_ref, *, n_heads, hd):
    g_pages = len(k_pages)
    page = ltri_ref.shape[0]
    lf = jnp.concatenate([r[...] for r in lf_pages], axis=0)
    both = _dot(jnp.concatenate(_split3(lf), axis=0), ltri_ref[...])
    gh = g_pages * n_heads
    both = both[:gh] + both[gh:2 * gh] + both[2 * gh:]
    inner = both[:, :page]
    total = both[:, page:]
    carry = car_ref[...]
    bias = [None] * g_pages
    for r in reversed(range(g_pages)):
        bias[r] = inner[r * n_heads:(r + 1) * n_heads] + carry
        carry = carry + total[r * n_heads:(r + 1) * n_heads]
    car_ref[...] = carry

    s_rows = [[None] * n_heads for _ in range(g_pages)]
    for h in range(n_heads):
        hs = slice(h * hd, (h + 1) * hd)
        qh = qcol_ref[0, hs, :]
        for r in range(g_pages):
            s_rows[r][h] = jnp.sum(k_pages[r][hs, :] * qh, axis=0, keepdims=True)
    s = [jnp.concatenate(s_rows[r], axis=0) + bias[r] * LOG2E for r in range(g_pages)]
    m_prev = m_ref[...]
    m_next = m_prev
    for r in range(g_pages):
        m_next = jnp.maximum(m_next, s[r])
    m_next = jnp.broadcast_to(jnp.max(m_next, axis=-1, keepdims=True), m_prev.shape)
    alpha = jnp.exp2(m_prev - m_next)
    p = [jnp.exp2(s[r] - m_next) for r in range(g_pages)]
    l_ref[...] = alpha * l_ref[...] + sum(p[1:], p[0])
    m_ref[...] = m_next
    for r in range(g_pages):
        w_ref[r] = p[r]
    w_ref[g_pages] = alpha
    return _order_token(alpha)


def _decode_values(v_pages, w_ref, acc_ref, token, *, n_heads, hd):
    g_pages = len(v_pages)
    rescale = w_ref[g_pages] + token
    for h in range(n_heads):
        hs = slice(h * hd, (h + 1) * hd)
        pv = v_pages[0][hs, :] * w_ref[0, h:h + 1, :]
        for r in range(1, g_pages):
            pv = pv + v_pages[r][hs, :] * w_ref[r, h:h + 1, :]
        acc_ref[hs, :] = rescale[h:h + 1, :] * acc_ref[hs, :] + pv
    return _order_token(acc_ref[0:8, :])


def _decode_finish(q_ref, kn_ref, vn_ref, m_ref, l_ref, acc_ref, o_ref, *, n_heads, hd):
    width = n_heads * hd
    row = lax.broadcasted_iota(jnp.int32, (n_heads, width), 0)
    col = lax.broadcasted_iota(jnp.int32, (n_heads, width), 1)
    head_mask = (col >= row * hd) & (col < (row + 1) * hd)

    def head_row(x):
        return jnp.sum(jnp.where(head_mask, jnp.broadcast_to(x, (n_heads, width)), 0.0),
                       axis=0, keepdims=True)

    qbd = jnp.where(head_mask, jnp.broadcast_to(q_ref[0], (n_heads, width)), 0.0)
    s_new = jnp.sum(qbd * kn_ref[0], axis=-1, keepdims=True)
    m_old = m_ref[:, :1]
    m_fin = jnp.maximum(m_old, s_new)
    a_fin = jnp.exp2(m_old - m_fin)
    p_new = jnp.exp2(s_new - m_fin)
    l_fin = a_fin * jnp.sum(l_ref[...], axis=-1, keepdims=True) + p_new
    past = jnp.sum(acc_ref[...].T, axis=0, keepdims=True)
    out = (head_row(a_fin) * past + head_row(p_new) * vn_ref[0]) / head_row(l_fin)
    o_ref[0] = out.astype(o_ref.dtype)


def _ffn_decode_kernel(pt_ref, x_ref, g_ref, wgu_ref, wd_ref, qcol_ref, q_ref, kn_ref, vn_ref, fn_ref,
                       ltri_ref, ck_hbm, cv_hbm, clf_hbm, *refs,
                       d_ff, chunk, down_block, b0, g_pages, n_chunks, steps_per_sample, n_heads, hd, with_ple):
    if with_ple:
        pe_ref, pg_ref, wpg_ref, wpp_ref = refs[:4]
        refs = refs[4:]
    o_ref, a_ref, act_ref, kbuf, vbuf, lfbuf, sem, m_ref, l_ref, acc_ref, car_ref, w_ref = refs
    step = pl.program_id(0)
    n_steps = pl.num_programs(0)
    n_groups = steps_per_sample * n_chunks

    def page_ids(st, c):
        b = b0 + st // steps_per_sample
        first = (n_groups - 1 - ((st % steps_per_sample) * n_chunks + c)) * g_pages
        return [pt_ref[b, first + r] for r in range(g_pages)]

    def k_copies(st, c):
        slot = c % n_slots
        out = []
        for r, pid in enumerate(page_ids(st, c)):
            out.append(pltpu.make_async_copy(ck_hbm.at[pid], kbuf.at[slot, r], sem.at[slot, 0]))
            out.append(pltpu.make_async_copy(clf_hbm.at[pid], lfbuf.at[slot, r], sem.at[slot, 2]))
        return out

    def v_copies(st, c):
        slot = c % n_slots
        return [pltpu.make_async_copy(cv_hbm.at[pid], vbuf.at[slot, r], sem.at[slot, 1])
                for r, pid in enumerate(page_ids(st, c))]

    n_slots = kbuf.shape[0]
    k_ahead, v_ahead = n_slots - 1, n_slots - 2

    priority_of = {k_copies: 0, v_copies: 1}

    def start_ahead(copies_of, c):
        if c < n_chunks:
            for cp in copies_of(step, c):
                cp.start(priority=priority_of[copies_of])
        else:
            @pl.when(step + 1 < n_steps)
            def _():
                for cp in copies_of(step + 1, c - n_chunks):
                    cp.start(priority=priority_of[copies_of])

    @pl.when(step == 0)
    def _():
        for c in range(k_ahead):
            for cp in k_copies(step, c):
                cp.start(priority=priority_of[k_copies])
        for c in range(v_ahead):
            for cp in v_copies(step, c):
                cp.start(priority=priority_of[v_copies])

    @pl.when(step % steps_per_sample == 0)
    def _():
        _decode_init(fn_ref, m_ref, l_ref, acc_ref, car_ref)

    def values(c, token):
        return _decode_values([vbuf.at[c % n_slots, r] for r in range(g_pages)], w_ref.at[c % 2], acc_ref,
                              token, n_heads=n_heads, hd=hd)

    def decode(c):
        slot = c % n_slots
        start_ahead(k_copies, c + k_ahead)
        start_ahead(v_copies, c + v_ahead)
        for cp in k_copies(step, c):
            cp.wait()
        if c > 0:
            for cp in v_copies(step, c - 1):
                cp.wait()
        token = _decode_scores([kbuf.at[slot, r] for r in range(g_pages)],
                               [lfbuf.at[slot, r] for r in range(g_pages)],
                               qcol_ref, ltri_ref, m_ref, l_ref, car_ref, w_ref.at[c % 2],
                               n_heads=n_heads, hd=hd)
        return values(c - 1, token) if c > 0 else token

    x = x_ref[...]
    h = _rms(x, g_ref[...]).astype(BF16)

    def waits_for(token, val):
        if token is None:
            return val
        return val + jnp.tile(token, (val.shape[0] // token.shape[0], val.shape[1] // token.shape[1]))

    def up(c, token=None):
        a = _dot(h, wgu_ref[:, c * chunk:(c + 1) * chunk])
        b = _dot(h, wgu_ref[:, d_ff + c * chunk:d_ff + (c + 1) * chunk])
        act_ref[:, c * chunk:(c + 1) * chunk] = (jax.nn.silu(a) * waits_for(token, b)).astype(BF16)

    def down(nb, token=None):
        cols = slice(nb * down_block, (nb + 1) * down_block)
        o_ref[:, cols] = waits_for(token, x[:, cols]) + 0.5 * _dot(act_ref[...], wd_ref[:, cols])

    def ple(token=None):
        x3 = waits_for(token, o_ref[...])
        gate = jax.nn.sigmoid(_dot(_rms(x3, pg_ref[...]).astype(BF16), wpg_ref[...]))
        o_ref[...] = x3 + gate * _dot(pe_ref[...].astype(BF16), wpp_ref[...])

    units = ([functools.partial(up, c) for c in range(d_ff // chunk)]
             + [functools.partial(down, nb) for nb in range(x.shape[1] // down_block)]
             + ([ple] if with_ple else []))
    for u, unit in enumerate(units):
        token = None
        for c in range(n_chunks):
            if (c * len(units)) // n_chunks == u:
                token = decode(c)
        unit(token) if token is not None else unit()
    for cp in v_copies(step, n_chunks - 1):
        cp.wait()
    values(n_chunks - 1, _order_token(w_ref[(n_chunks - 1) % 2, g_pages]))

    @pl.when(step % steps_per_sample == steps_per_sample - 1)
    def _():
        _decode_finish(q_ref, kn_ref, vn_ref, m_ref, l_ref, acc_ref, a_ref, n_heads=n_heads, hd=hd)


def _ffn_decode_call(x, g, wgu, wd, tm, page_table, b0, n_samples, qcol, q, kn, vn, fn,
                     cache_kt, cache_vt, cache_lf_t, g_pages, n_heads, hd, ple=None):
    n, d = x.shape
    d_ff = wd.shape[0]
    chunk = _ffn_chunk(d_ff)
    n_steps = n // tm
    n_pages = page_table.shape[1]
    n_pool, width, page = cache_kt.shape
    assert n_steps % n_samples == 0
    steps_per_sample = n_steps // n_samples
    assert n_pages % (steps_per_sample * g_pages) == 0
    n_chunks = n_pages // (steps_per_sample * g_pages)
    down_block = 2 * MXU_TILE if d % (2 * MXU_TILE) == 0 else d
    n_slots = CACHE_RING_SLOTS
    assert n_chunks % n_slots == 0, "a chunk's ring slot must not depend on the grid step"
    ltri = jnp.asarray(np.concatenate([np.tril(np.ones((page, page), np.float32), -1),
                                       np.ones((page, page), np.float32)], axis=1), BF16)
    sample = lambda i, pt: (b0 + i // steps_per_sample, 0, 0)
    const = lambda shape: pl.BlockSpec(shape, lambda i, pt: (0,) * len(shape), pipeline_mode=pl.Buffered(1))
    row_spec = pl.BlockSpec((1, 1, width), sample)
    hbm = pl.BlockSpec(memory_space=pl.ANY)
    ple_specs, ple_args = [], ()
    if ple is not None:
        ple_args = tuple(ple)
        ple_specs = ([pl.BlockSpec((tm, ple[0].shape[1]), lambda i, pt: (i, 0))]
                     + [const(t.shape) for t in ple[1:]])
    return pl.pallas_call(
        functools.partial(_ffn_decode_kernel, d_ff=d_ff, chunk=chunk, down_block=down_block, b0=b0, g_pages=g_pages,
                          n_chunks=n_chunks, steps_per_sample=steps_per_sample, n_heads=n_heads, hd=hd,
                          with_ple=ple is not None),
        out_shape=(jax.ShapeDtypeStruct((n, d), F32), jax.ShapeDtypeStruct((n_samples, 1, width), BF16)),
        grid_spec=pltpu.PrefetchScalarGridSpec(
            num_scalar_prefetch=1,
            grid=(n_steps,),
            in_specs=[pl.BlockSpec((tm, d), lambda i, pt: (i, 0)),
                      const(g.shape), const(wgu.shape), const(wd.shape),
                      pl.BlockSpec((1, width, page), sample), row_spec, row_spec, row_spec,
                      pl.BlockSpec((1, n_heads, page), sample), const(ltri.shape), hbm, hbm, hbm] + ple_specs,
            out_specs=(pl.BlockSpec((tm, d), lambda i, pt: (i, 0)),
                       pl.BlockSpec((1, 1, width), lambda i, pt: (i // steps_per_sample, 0, 0))),
            scratch_shapes=[pltpu.VMEM((tm, d_ff), BF16),
                            pltpu.VMEM((n_slots, g_pages, width, page), F32),
                            pltpu.VMEM((n_slots, g_pages, width, page), F32),
                            pltpu.VMEM((n_slots, g_pages, n_heads, page), F32),
                            pltpu.SemaphoreType.DMA((n_slots, 3)),
                            pltpu.VMEM((n_heads, page), F32), pltpu.VMEM((n_heads, page), F32),
                            pltpu.VMEM((width, page), F32), pltpu.VMEM((n_heads, page), F32),
                            pltpu.VMEM((2, g_pages + 1, n_heads, page), F32)]),
        compiler_params=_params(1),
        name="ffn_decode",
    )(page_table, x, g, wgu, wd, qcol, q, kn, vn, fn, ltri, cache_kt, cache_vt, cache_lf_t, *ple_args)


def _merge_kernel(x_ref, a_ref, mg_ref, g_ref, wga_ref, wgb_ref, wpa_ref, wpg_ref, wo_ref, o_ref):
    x = x_ref[...]
    h = _rms(x, g_ref[...]).astype(BF16)
    ga = jax.nn.sigmoid(_dot(h, wga_ref[...]))
    gb = jax.nn.sigmoid(_dot(h, wgb_ref[...]))
    merged = ga * _dot(a_ref[...], wpa_ref[...]) + gb * _dot(mg_ref[...], wpg_ref[...])
    o_ref[...] = x + _dot(merged.astype(BF16), wo_ref[...])


def _merge_call(x, a, mg, g, wga, wgb, wpa, wpg, wo, tm):
    n, d = x.shape
    w = a.shape[1]
    row = lambda cols: pl.BlockSpec((tm, cols), lambda i: (i, 0))
    return pl.pallas_call(
        _merge_kernel,
        out_shape=jax.ShapeDtypeStruct((n, d), F32),
        grid=(n // tm,),
        in_specs=[row(d), row(w), row(mg.shape[1])] + [_const_spec(t.shape) for t in (g, wga, wgb, wpa, wpg, wo)],
        out_specs=row(d),
        compiler_params=_params(1),
        name="merge_out",
    )(x, a, mg, g, wga, wgb, wpa, wpg, wo)


def _ple_kernel(x_ref, p_ref, g_ref, wg_ref, wp_ref, o_ref):
    x = x_ref[...]
    gate = jax.nn.sigmoid(_dot(_rms(x, g_ref[...]).astype(BF16), wg_ref[...]))
    o_ref[...] = x + gate * _dot(p_ref[...].astype(BF16), wp_ref[...])


def _ple_call(x, p, g, wg, wp, tm):
    n, d = x.shape
    row = lambda cols: pl.BlockSpec((tm, cols), lambda i: (i, 0))
    return pl.pallas_call(
        _ple_kernel,
        out_shape=jax.ShapeDtypeStruct((n, d), F32),
        grid=(n // tm,),
        in_specs=[row(d), row(p.shape[1])] + [_const_spec(t.shape) for t in (g, wg, wp)],
        out_specs=row(d),
        compiler_params=_params(1),
        name="ple",
    )(x, p, g, wg, wp)


class _Tiles(NamedTuple):
    rows: int
    merge_rows: int
    attn_block: int
    cache_pages: int


def _tiles(n_tokens, seq):
    rows = 512
    merge_rows = 1024 if n_tokens % 1024 == 0 else rows
    attn_block = 512
    assert n_tokens % rows == 0 and seq % rows == 0 and seq % attn_block == 0
    return _Tiles(rows=rows, merge_rows=merge_rows, attn_block=attn_block, cache_pages=8)


def _block_diag_mean(width, group):
    idx = np.arange(width) // group
    return jnp.asarray((idx[:, None] == idx[None, :]).astype(np.float32) / group, BF16)


def _forget_selectors(n_heads, hd):
    width = n_heads * hd
    selq = np.zeros((LANES, width), np.float32)
    selk = np.zeros((LANES, width), np.float32)
    for h in range(n_heads):
        for p in range(3):
            selq[p * PIECE_STRIDE + h, h * hd + p] = 1.0
            selk[p * PIECE_STRIDE + h, h * hd + 3 + p] = -1.0
            selq[n_heads, h * hd + 3 + p] = 1.0
            selk[n_heads, h * hd + p] = 1.0
    return jnp.asarray(selq.T, BF16), jnp.asarray(selk, BF16)


def kernel(x_prompt, x_sample, cache_k, cache_v, cache_logf, page_table, p_prompt, p_sample,
           ffn1_norm, ffn1_w_gu, ffn1_w_down, mix_norm, w_in, b_forget, q_norm, k_norm,
           gmlp_v_norm, w_spatial, b_spatial, w_proj_attn, w_proj_gmlp, w_out,
           ffn2_norm, ffn2_w_gu, ffn2_w_down, ple_norm, ple_w_gate, ple_w_proj):
    depth = ffn1_norm.shape[0]
    bsz, seq, d = x_prompt.shape
    db = x_sample.shape[0]
    assert x_sample.shape[1] == 1, "the sample group decodes one token per step"
    n_heads, hd = cache_k.shape[3], cache_k.shape[4]
    aw = n_heads * hd
    n_groups, chunk = w_spatial.shape[1], w_spatial.shape[2]
    gw = w_proj_gmlp.shape[1]
    gdim = gw // n_groups
    page = cache_k.shape[2]
    n_pool = cache_k.shape[1]
    assert hd == gdim and aw == gw and LANES % hd == 0 and n_heads < PIECE_STRIDE and page == LANES

    tiles = _tiles(bsz * seq, seq)
    tm, tq, g_pages = tiles.rows, tiles.attn_block, tiles.cache_pages
    scale = hd ** -0.5
    tri = jnp.asarray(np.tril(np.ones((tm, tm), np.float32)), BF16)
    bd = _block_diag_mean(math.gcd(aw, MXU_TILE), hd)
    selqt, selk = _forget_selectors(n_heads, hd)

    xp = x_prompt.reshape(bsz * seq, d)
    xs = x_sample.reshape(db, d)
    outs = {k: [] for k in ("kp", "vp", "fp", "ks", "vs", "fs", "gs")}
    for li in range(depth):
        bf = lambda t: t[li].astype(BF16)
        row = lambda t: t[li].reshape(1, -1).astype(F32)
        tile_h = lambda t, mult=1.0: jnp.tile(t[li].astype(F32) * mult, n_heads).reshape(1, -1)
        w1gu, w1d, w2gu, w2d = bf(ffn1_w_gu), bf(ffn1_w_down), bf(ffn2_w_gu), bf(ffn2_w_down)
        wi = w_in[li]
        o = 0
        wqkv = wi[:, o:o + 3 * aw].astype(BF16); o += 3 * aw
        wf = jnp.pad(wi[:, o:o + n_heads], ((0, 0), (0, LANES - n_heads))).astype(BF16); o += n_heads
        wugv = wi[:, o:o + 2 * gw].astype(BF16); o += 2 * gw
        wga = wi[:, o:o + d].astype(BF16); o += d
        wgb = wi[:, o:o + d].astype(BF16)
        bfg = jnp.pad(b_forget[li].astype(F32), (0, LANES - n_heads)).reshape(1, LANES)
        qg, kg, gvg = tile_h(q_norm, scale * LOG2E), tile_h(k_norm), tile_h(gmlp_v_norm)
        ws = bf(w_spatial)
        bs = jnp.repeat(b_spatial[li].astype(F32).T, gdim, axis=1)
        w00 = jnp.repeat(w_spatial[li, :, 0, 0].astype(F32), gdim).reshape(1, gw)
        b0 = jnp.repeat(b_spatial[li, :, 0].astype(F32), gdim).reshape(1, gw)
        wpa, wpg, wo = bf(w_proj_attn), bf(w_proj_gmlp), bf(w_out)
        plg, plp = bf(ple_w_gate), bf(ple_w_proj)
        mixer_w = (row(mix_norm), wqkv, wf, bfg, wugv, qg, kg, gvg, bd)

        x1s = _ffn_call(xs, row(ffn1_norm), w1gu, w1d, db)
        sconsts = mixer_w + (w00, b0)
        full = lambda cols: pl.BlockSpec((db, cols), lambda i: (0, 0))
        ssd = lambda cols, dt: jax.ShapeDtypeStruct((db, cols), dt)
        qs, ksn, vsn, lfs, gvs, mgs = pl.pallas_call(
            _mixer_sample_kernel,
            out_shape=(ssd(aw, F32), ssd(aw, F32), ssd(aw, F32), ssd(LANES, F32), ssd(gw, F32), ssd(gw, BF16)),
            grid=(1,),
            in_specs=[full(d)] + [_const_spec(t.shape) for t in sconsts],
            out_specs=tuple(full(c) for c in (aw, aw, aw, LANES, gw, gw)),
            compiler_params=_params(1),
            name="mixer_sample",
        )(x1s, *sconsts)
        fn = jnp.broadcast_to(lfs[:, :n_heads, None], (db, n_heads, page))
        qcol = jnp.broadcast_to(qs[:, :, None], (db, aw, page))
        cache_lf_t = jnp.swapaxes(cache_logf[li].astype(F32), 1, 2)
        cache_kt = jnp.transpose(cache_k[li], (0, 2, 3, 1)).reshape(n_pool, aw, page)
        cache_vt = jnp.transpose(cache_v[li], (0, 2, 3, 1)).reshape(n_pool, aw, page)
        decode_args = (qcol, qs.reshape(db, 1, aw), ksn.reshape(db, 1, aw), vsn.reshape(db, 1, aw), fn,
                       cache_kt, cache_vt, cache_lf_t, g_pages, n_heads, hd)
        half = db // 2

        n = bsz * seq
        x1, a_s0 = _ffn_decode_call(xp, row(ffn1_norm), w1gu, w1d, tm, page_table, 0, half, *decode_args)
        rowspec = lambda cols: pl.BlockSpec((tm, cols), lambda i: (i, 0))
        consts = mixer_w + (tri, selqt, selk, ws, bs)
        sds = lambda cols, dt: jax.ShapeDtypeStruct((n, cols), dt)
        tps = seq // tm
        tsd = lambda rows_, dt: jax.ShapeDtypeStruct((bsz, rows_, seq), dt)
        tspec = lambda rows_: pl.BlockSpec((1, rows_, tm), lambda i: (i // tps, 0, i % tps))
        qt, qat, k32t, kp, ka, v32t, vt, lfpt, mg = pl.pallas_call(
            functools.partial(_mixer_prompt_kernel, tiles_per_seq=tps, n_heads=n_heads, chunk=chunk),
            out_shape=(tsd(aw, BF16), tsd(aw, BF16), tsd(aw, F32), sds(aw, BF16), sds(aw, BF16),
                       tsd(aw, F32), tsd(aw, BF16), tsd(n_heads, F32), sds(gw, BF16)),
            grid=(n // tm,),
            in_specs=[rowspec(d)] + [_const_spec(t.shape) for t in consts],
            out_specs=(tspec(aw), tspec(aw), tspec(aw), rowspec(aw), rowspec(aw),
                       tspec(aw), tspec(aw), tspec(n_heads), rowspec(gw)),
            scratch_shapes=[pltpu.VMEM((1, LANES), F32)],
            compiler_params=_params(1),
            name="mixer_prompt",
        )(x1, *consts)
        r3 = lambda t: t.reshape(bsz, seq, aw)
        a = _attn_call(qt, qat, r3(kp), r3(ka), vt, tq, hd).reshape(n, aw)
        x2 = _merge_call(x1, a, mg, row(mix_norm), wga, wgb, wpa, wpg, wo, tiles.merge_rows)
        xp, a_s1 = _ffn_decode_call(x2, row(ffn2_norm), w2gu, w2d, tm, page_table, half, db - half,
                                    *decode_args,
                                    ple=(p_prompt[li].reshape(n, -1), row(ple_norm), plg, plp))
        outs["kp"].append(k32t.reshape(bsz, n_heads, hd, seq).transpose(0, 3, 1, 2))
        outs["vp"].append(v32t.reshape(bsz, n_heads, hd, seq).transpose(0, 3, 1, 2))
        outs["fp"].append(lfpt.transpose(0, 2, 1))

        a_s = jnp.concatenate([a_s0, a_s1], axis=0).reshape(db, aw)
        x2s = _merge_call(x1s, a_s, mgs, row(mix_norm), wga, wgb, wpa, wpg, wo, db)
        x3s = _ffn_call(x2s, row(ffn2_norm), w2gu, w2d, db)
        xs = _ple_call(x3s, p_sample[li].reshape(db, -1), row(ple_norm), plg, plp, db)
        outs["ks"].append(ksn.reshape(db, 1, n_heads, hd))
        outs["vs"].append(vsn.reshape(db, 1, n_heads, hd))
        outs["fs"].append(lfs[:, :n_heads].reshape(db, 1, n_heads))
        outs["gs"].append(gvs.reshape(db, 1, n_groups, gdim))

    st = lambda key: jnp.stack(outs[key])
    return (xp.reshape(bsz, seq, d), xs.reshape(db, 1, d), st("kp"), st("vp"), st("fp"),
            st("ks"), st("vs"), st("fs"), st("gs"))
```

```python
import functools
import math
from typing import NamedTuple

import jax
import jax.numpy as jnp
import numpy as np
from jax import lax
from jax.experimental import pallas as pl
from jax.experimental.pallas import tpu as pltpu

EPS = 1e-6
NEG_INF = -1e30
LOG2E = 1.4426950408889634
LANES = 128
BF16_ROWS = 16
MXU_TILE = 256
PIECE_STRIDE = 16
CACHE_RING_SLOTS = 4
F32 = jnp.float32
BF16 = jnp.bfloat16
VMEM_LIMIT = 56 * 1024 * 1024


def _dot(a, b):
    return jnp.dot(a, b, preferred_element_type=F32)


def _rms(x, g):
    ms = jnp.mean(x * x, axis=-1, keepdims=True)
    return x * lax.rsqrt(ms + EPS) * g


def _group_rms(x, g, bd):
    x2 = (x * x).astype(BF16)
    w = bd.shape[0]
    ms = jnp.concatenate([_dot(x2[:, i * w:(i + 1) * w], bd) for i in range(x.shape[1] // w)], axis=-1)
    return x * lax.rsqrt(ms + EPS) * g


def _split3(x):
    hi = x.astype(BF16)
    r = x - hi.astype(F32)
    mid = r.astype(BF16)
    lo = (r - mid.astype(F32)).astype(BF16)
    return hi, mid, lo


def _pack3(x, stride):
    hi, mid, lo = (p.astype(F32) for p in _split3(x))
    return hi + pltpu.roll(mid, stride, axis=1) + pltpu.roll(lo, 2 * stride, axis=1)


def _const_spec(shape):
    nd = len(shape)
    return pl.BlockSpec(shape, lambda *_: (0,) * nd, pipeline_mode=pl.Buffered(1))


def _params(n_axes):
    return pltpu.CompilerParams(dimension_semantics=("arbitrary",) * n_axes,
                                vmem_limit_bytes=VMEM_LIMIT)


def _ffn_chunk(d_ff):
    return MXU_TILE if d_ff % MXU_TILE == 0 else LANES


def _ffn_kernel(x_ref, g_ref, wgu_ref, wd_ref, o_ref, act_ref, *, d_ff, chunk):
    x = x_ref[...]
    h = _rms(x, g_ref[...]).astype(BF16)
    for c in range(d_ff // chunk):
        a = _dot(h, wgu_ref[:, c * chunk:(c + 1) * chunk])
        b = _dot(h, wgu_ref[:, d_ff + c * chunk:d_ff + (c + 1) * chunk])
        act_ref[:, c * chunk:(c + 1) * chunk] = (jax.nn.silu(a) * b).astype(BF16)
    o_ref[...] = x + 0.5 * _dot(act_ref[...], wd_ref[...])


def _ffn_call(x, g, wgu, wd, tm):
    n, d = x.shape
    d_ff = wd.shape[0]
    chunk = _ffn_chunk(d_ff)
    return pl.pallas_call(
        functools.partial(_ffn_kernel, d_ff=d_ff, chunk=chunk),
        out_shape=jax.ShapeDtypeStruct((n, d), F32),
        grid=(n // tm,),
        in_specs=[pl.BlockSpec((tm, d), lambda i: (i, 0)),
                  _const_spec(g.shape), _const_spec(wgu.shape), _const_spec(wd.shape)],
        out_specs=pl.BlockSpec((tm, d), lambda i: (i, 0)),
        scratch_shapes=[pltpu.VMEM((tm, d_ff), BF16)],
        compiler_params=_params(1),
        name="ffn",
    )(x, g, wgu, wd)


def _mixer_common(x_ref, g_ref, wqkv_ref, wf_ref, bf_ref, wugv_ref, qg_ref, kg_ref, gvg_ref, bd_ref):
    h = _rms(x_ref[...], g_ref[...]).astype(BF16)
    aw = qg_ref.shape[1]
    qkv = _dot(h, wqkv_ref[...])
    q = _group_rms(qkv[:, :aw], qg_ref[...], bd_ref[...])
    k = _group_rms(qkv[:, aw:2 * aw], kg_ref[...], bd_ref[...])
    v = qkv[:, 2 * aw:]
    logf = jax.nn.log_sigmoid(_dot(h, wf_ref[...]) + bf_ref[...])
    ugv = _dot(h, wugv_ref[...])
    gw = gvg_ref.shape[1]
    u = jax.nn.gelu(ugv[:, :gw])
    gv = _group_rms(jax.nn.gelu(ugv[:, gw:]), gvg_ref[...], bd_ref[...])
    return q, k, v, logf, u, gv


def _mixer_prompt_kernel(x_ref, g_ref, wqkv_ref, wf_ref, bf_ref, wugv_ref, qg_ref, kg_ref, gvg_ref,
                         bd_ref, tri_ref, selqt_ref, selk_ref, ws_ref, bs_ref,
                         qt_ref, qat_ref, k_ref, kp_ref, ka_ref, v_ref, vt_ref, lf_ref, mg_ref,
                         carry_ref, *, tiles_per_seq, n_heads, chunk):
    q, k, v, logf, u, gv = _mixer_common(x_ref, g_ref, wqkv_ref, wf_ref, bf_ref, wugv_ref,
                                         qg_ref, kg_ref, gvg_ref, bd_ref)
    tm = q.shape[0]
    qt_ref[0] = q.T.astype(BF16)
    k_ref[0] = k.T
    kp_ref[...] = k.astype(BF16)
    v_t = v.T
    v_ref[0] = v_t
    vt_ref[0] = v_t.astype(BF16)
    lf_ref[0] = logf.T[:n_heads]

    @pl.when(pl.program_id(0) % tiles_per_seq == 0)
    def _():
        carry_ref[...] = jnp.zeros_like(carry_ref)

    lane = lax.broadcasted_iota(jnp.int32, logf.shape, 1)
    lf = jnp.where(lane < n_heads, logf, 0.0)
    cs = _dot(tri_ref[...], _pack3(lf, PIECE_STRIDE).astype(BF16))
    cs = cs + pltpu.roll(cs, LANES - PIECE_STRIDE, axis=1) + pltpu.roll(cs, LANES - 2 * PIECE_STRIDE, axis=1)
    f_cum = jnp.where(lane < n_heads, cs, 0.0) + carry_ref[...]
    carry_ref[...] = f_cum[tm - 1:tm, :]
    f_ext = _pack3(jnp.where(lane == n_heads, 1.0, f_cum * LOG2E), PIECE_STRIDE)
    ka_ref[...] = _dot(f_ext.astype(BF16), selk_ref[...]).astype(BF16)
    qat_ref[0] = _dot(selqt_ref[...], f_ext.T.astype(BF16)).astype(BF16)

    gvb = gv.astype(BF16)
    r_i = lax.broadcasted_iota(jnp.int32, (chunk, chunk), 0)
    c_i = lax.broadcasted_iota(jnp.int32, (chunk, chunk), 1)
    lane_c = lax.broadcasted_iota(jnp.int32, (chunk, tm // chunk * LANES), 1) & (LANES - 1)
    n_groups = ws_ref.shape[0]
    gdim = gvb.shape[1] // n_groups
    per_blk = LANES // gdim
    ws = [jnp.where(c_i <= r_i, ws_ref[g], jnp.zeros((), BF16)) for g in range(n_groups)]
    n_c = tm // chunk
    for jb in range(gvb.shape[1] // LANES):
        cols = slice(jb * LANES, (jb + 1) * LANES)
        blk = jnp.concatenate([gvb[c * chunk:(c + 1) * chunk, cols] for c in range(n_c)], axis=1)
        sp = _dot(ws[jb * per_blk], blk)
        for gi in range(1, per_blk):
            sp = jnp.where(lane_c >= gi * gdim, _dot(ws[jb * per_blk + gi], blk), sp)
        for c in range(n_c):
            rows = slice(c * chunk, (c + 1) * chunk)
            mg_ref[rows, cols] = (u[rows, cols] * (sp[:, c * LANES:(c + 1) * LANES] + bs_ref[:, cols])).astype(BF16)


def _mixer_sample_kernel(x_ref, g_ref, wqkv_ref, wf_ref, bf_ref, wugv_ref, qg_ref, kg_ref, gvg_ref,
                         bd_ref, w00_ref, b0_ref,
                         q_ref, k_ref, v_ref, lf_ref, gv_ref, mg_ref):
    q, k, v, logf, u, gv = _mixer_common(x_ref, g_ref, wqkv_ref, wf_ref, bf_ref, wugv_ref,
                                         qg_ref, kg_ref, gvg_ref, bd_ref)
    q_ref[...] = q
    k_ref[...] = k
    v_ref[...] = v
    lf_ref[...] = logf
    gv_ref[...] = gv
    mg_ref[...] = (u * (gv * w00_ref[...] + b0_ref[...])).astype(BF16)


def _attn_kernel(qt_ref, qat_ref, kp_ref, ka_ref, vt_ref, o_ref, m_ref, acc_ref,
                 s_ref, p_ref, al_ref, bm_ref, *, tq, hd):
    n_sub = LANES // hd
    row_q = lax.broadcasted_iota(jnp.int32, (2 * LANES, tq), 0) & (LANES - 1)
    head_rows = [(row_q >= h * hd) & (row_q < (h + 1) * hd) for h in range(n_sub)]

    def query_block(i, carry):
        q0 = pl.multiple_of(i * tq, tq)
        qcat = jnp.concatenate([qt_ref[0, :, pl.ds(q0, tq)], qat_ref[0, :, pl.ds(q0, tq)]], axis=0)
        qs = [jnp.where(head_rows[h], qcat, jnp.zeros((), BF16)) for h in range(n_sub)]
        m_ref[...] = jnp.full_like(m_ref, NEG_INF)
        acc_ref[...] = jnp.zeros_like(acc_ref)
        p_ref[1] = jnp.zeros_like(p_ref[1])
        al_ref[1] = jnp.ones_like(al_ref[1])

        def scores(t, slot):
            start = pl.multiple_of(t * tq, tq)
            kc = jnp.concatenate([kp_ref[0, pl.ds(start, tq), :], ka_ref[0, pl.ds(start, tq), :]], axis=-1)
            for h in range(n_sub):
                s = _dot(kc, qs[h])
                s_ref[slot, h] = s
                bm_ref[slot, h] = jnp.max(s, axis=0, keepdims=True)

        def softmax(slot, masked):
            for h in range(n_sub):
                s = s_ref[slot, h]
                if masked:
                    key = lax.broadcasted_iota(jnp.int32, s.shape, 0)
                    qry = lax.broadcasted_iota(jnp.int32, s.shape, 1)
                    s = jnp.where(key <= qry, s, NEG_INF)
                    blk_max = jnp.max(s, axis=0, keepdims=True)
                else:
                    blk_max = bm_ref[slot, h]
                m_prev = m_ref[h]
                m_next = jnp.maximum(m_prev, blk_max)
                alpha = jnp.exp2(m_prev - m_next)
                p = jnp.exp2(s - m_next)
                m_ref[h] = m_next
                al_ref[slot, h] = alpha
                p_ref[slot, h] = p.astype(BF16)

        def values(t, slot):
            start = pl.multiple_of(jnp.maximum(t, 0) * tq, tq)
            for h in range(n_sub):
                v_t = jnp.concatenate([vt_ref[0, h * hd:(h + 1) * hd, pl.ds(start, tq)],
                                       jnp.ones((BF16_ROWS, tq), BF16)], axis=0)
                acc_ref[h] = al_ref[slot, h] * acc_ref[h] + _dot(v_t, p_ref[slot, h])

        UNROLL = 2

        def stage(t, slot):
            scores(t + 1, 1 - slot)
            softmax(slot, False)
            values(t - 1, 1 - slot)

        def body(u, c):
            for k in range(UNROLL):
                stage(UNROLL * u + k, k % 2)
            return c

        scores(0, 0)
        lax.fori_loop(0, i // UNROLL, body, 0)

        for rem in range(UNROLL):
            @pl.when(i % UNROLL == rem)
            def _(rem=rem):
                for k in range(rem):
                    stage(i - rem + k, k % 2)
                softmax(rem % 2, True)
                values(i - 1, 1 - rem % 2)
                values(i, rem % 2)

        out_t = jnp.concatenate([acc_ref[h, :hd] / acc_ref[h, hd:hd + 1] for h in range(n_sub)], axis=0)
        o_ref[0, pl.ds(q0, tq), :] = out_t.T.astype(o_ref.dtype)
        return carry

    lax.fori_loop(0, o_ref.shape[1] // tq, query_block, 0)


def _attn_call(qt, qat, kp, ka, vt, tq, hd):
    b, s, w = kp.shape
    n_sub = LANES // hd
    t_spec = pl.BlockSpec((1, LANES, s), lambda bi, hp: (bi, hp, 0))
    k_spec = pl.BlockSpec((1, s, LANES), lambda bi, hp: (bi, 0, hp))
    return pl.pallas_call(
        functools.partial(_attn_kernel, tq=tq, hd=hd),
        out_shape=jax.ShapeDtypeStruct((b, s, w), BF16),
        grid=(b, w // LANES),
        in_specs=[t_spec, t_spec, k_spec, k_spec, t_spec],
        out_specs=k_spec,
        scratch_shapes=[pltpu.VMEM((n_sub, 1, tq), F32),
                        pltpu.VMEM((n_sub, hd + BF16_ROWS, tq), F32), pltpu.VMEM((2, n_sub, tq, tq), F32),
                        pltpu.VMEM((2, n_sub, tq, tq), BF16), pltpu.VMEM((2, n_sub, 1, tq), F32),
                        pltpu.VMEM((2, n_sub, 1, tq), F32)],
        compiler_params=_params(2),
        name="fox_attn_prompt",
    )(qt, qat, kp, ka, vt)


def _decode_init(fn_ref, m_ref, l_ref, acc_ref, car_ref):
    m_ref[...] = jnp.full_like(m_ref, NEG_INF)
    l_ref[...] = jnp.zeros_like(l_ref)
    acc_ref[...] = jnp.zeros_like(acc_ref)
    car_ref[...] = jnp.broadcast_to(fn_ref[0], car_ref.shape)


def _order_token(x):
    bits = pltpu.bitcast(x, jnp.uint32)
    return pltpu.bitcast(lax.shift_right_logical(bits, jnp.uint32(32)), F32)


def _decode_scores(k_pages, lf_pages, qcol_ref, ltri_ref, m_ref, l_ref, car_ref, w_ref, *, n_heads, hd):
    g_pages = len(k_pages)
    page = ltri_ref.shape[0]
    lf = jnp.concatenate([r[...] for r in lf_pages], axis=0)
    both = _dot(jnp.concatenate(_split3(lf), axis=0), ltri_ref[...])
    gh = g_pages * n_heads
    both = both[:gh] + both[gh:2 * gh] + both[2 * gh:]
    inner = both[:, :page]
    total = both[:, page:]
    carry = car_ref[...]
    bias = [None] * g_pages
    for r in reversed(range(g_pages)):
        bias[r] = inner[r * n_heads:(r + 1) * n_heads] + carry
        carry = carry + total[r * n_heads:(r + 1) * n_heads]
    car_ref[...] = carry

    s_rows = [[None] * n_heads for _ in range(g_pages)]
    for h in range(n_heads):
        hs = slice(h * hd, (h + 1) * hd)
        qh = qcol_ref[0, hs, :]
        for r in range(g_pages):
            s_rows[r][h] = jnp.sum(k_pages[r][hs, :] * qh, axis=0, keepdims=True)
    s = [jnp.concatenate(s_rows[r], axis=0) + bias[r] * LOG2E for r in range(g_pages)]
    m_prev = m_ref[...]
    m_next = m_prev
    for r in range(g_pages):
        m_next = jnp.maximum(m_next, s[r])
    m_next = jnp.broadcast_to(jnp.max(m_next, axis=-1, keepdims=True), m_prev.shape)
    alpha = jnp.exp2(m_prev - m_next)
    p = [jnp.exp2(s[r] - m_next) for r in range(g_pages)]
    l_ref[...] = alpha * l_ref[...] + sum(p[1:], p[0])
    m_ref[...] = m_next
    for r in range(g_pages):
        w_ref[r] = p[r]
    w_ref[g_pages] = alpha
    return _order_token(alpha)


def _decode_values(v_pages, w_ref, acc_ref, token, *, n_heads, hd):
    g_pages = len(v_pages)
    rescale = w_ref[g_pages] + token
    for h in range(n_heads):
        hs = slice(h * hd, (h + 1) * hd)
        pv = v_pages[0][hs, :] * w_ref[0, h:h + 1, :]
        for r in range(1, g_pages):
            pv = pv + v_pages[r][hs, :] * w_ref[r, h:h + 1, :]
        acc_ref[hs, :] = rescale[h:h + 1, :] * acc_ref[hs, :] + pv
    return _order_token(acc_ref[0:8, :])


def _decode_finish(q_ref, kn_ref, vn_ref, m_ref, l_ref, acc_ref, o_ref, *, n_heads, hd):
    width = n_heads * hd
    row = lax.broadcasted_iota(jnp.int32, (n_heads, width), 0)
    col = lax.broadcasted_iota(jnp.int32, (n_heads, width), 1)
    head_mask = (col >= row * hd) & (col < (row + 1) * hd)

    def head_row(x):
        return jnp.sum(jnp.where(head_mask, jnp.broadcast_to(x, (n_heads, width)), 0.0),
                       axis=0, keepdims=True)

    qbd = jnp.where(head_mask, jnp.broadcast_to(q_ref[0], (n_heads, width)), 0.0)
    s_new = jnp.sum(qbd * kn_ref[0], axis=-1, keepdims=True)
    m_old = m_ref[:, :1]
    m_fin = jnp.maximum(m_old, s_new)
    a_fin = jnp.exp2(m_old - m_fin)
    p_new = jnp.exp2(s_new - m_fin)
    l_fin = a_fin * jnp.sum(l_ref[...], axis=-1, keepdims=True) + p_new
    past = jnp.sum(acc_ref[...].T, axis=0, keepdims=True)
    out = (head_row(a_fin) * past + head_row(p_new) * vn_ref[0]) / head_row(l_fin)
    o_ref[0] = out.astype(o_ref.dtype)


def _ffn_decode_kernel(pt_ref, x_ref, g_ref, wgu_ref, wd_ref, qcol_ref, q_ref, kn_ref, vn_ref, fn_ref,
                       ltri_ref, ck_hbm, cv_hbm, clf_hbm, *refs,
                       d_ff, chunk, down_block, b0, g_pages, n_chunks, steps_per_sample, n_heads, hd, with_ple):
    if with_ple:
        pe_ref, pg_ref, wpg_ref, wpp_ref = refs[:4]
        refs = refs[4:]
    o_ref, a_ref, act_ref, kbuf, vbuf, lfbuf, sem, m_ref, l_ref, acc_ref, car_ref, w_ref = refs
    step = pl.program_id(0)
    n_steps = pl.num_programs(0)
    n_groups = steps_per_sample * n_chunks

    def page_ids(st, c):
        b = b0 + st // steps_per_sample
        first = (n_groups - 1 - ((st % steps_per_sample) * n_chunks + c)) * g_pages
        return [pt_ref[b, first + r] for r in range(g_pages)]

    def k_copies(st, c):
        slot = c % n_slots
        out = []
        for r, pid in enumerate(page_ids(st, c)):
            out.append(pltpu.make_async_copy(ck_hbm.at[pid], kbuf.at[slot, r], sem.at[slot, 0]))
            out.append(pltpu.make_async_copy(clf_hbm.at[pid], lfbuf.at[slot, r], sem.at[slot, 2]))
        return out

    def v_copies(st, c):
        slot = c % n_slots
        return [pltpu.make_async_copy(cv_hbm.at[pid], vbuf.at[slot, r], sem.at[slot, 1])
                for r, pid in enumerate(page_ids(st, c))]

    n_slots = kbuf.shape[0]
    k_ahead, v_ahead = n_slots - 1, n_slots - 2

    priority_of = {k_copies: 1, v_copies: 1}

    def start_ahead(copies_of, c):
        if c < n_chunks:
            for cp in copies_of(step, c):
                cp.start(priority=priority_of[copies_of])
        else:
            @pl.when(step + 1 < n_steps)
            def _():
                for cp in copies_of(step + 1, c - n_chunks):
                    cp.start(priority=priority_of[copies_of])

    @pl.when(step == 0)
    def _():
        for c in range(k_ahead):
            for cp in k_copies(step, c):
                cp.start(priority=priority_of[k_copies])
        for c in range(v_ahead):
            for cp in v_copies(step, c):
                cp.start(priority=priority_of[v_copies])

    @pl.when(step % steps_per_sample == 0)
    def _():
        _decode_init(fn_ref, m_ref, l_ref, acc_ref, car_ref)

    def values(c, token):
        return _decode_values([vbuf.at[c % n_slots, r] for r in range(g_pages)], w_ref.at[c % 2], acc_ref,
                              token, n_heads=n_heads, hd=hd)

    def decode(c):
        slot = c % n_slots
        start_ahead(k_copies, c + k_ahead)
        start_ahead(v_copies, c + v_ahead)
        for cp in k_copies(step, c):
            cp.wait()
        if c > 0:
            for cp in v_copies(step, c - 1):
                cp.wait()
        token = _decode_scores([kbuf.at[slot, r] for r in range(g_pages)],
                               [lfbuf.at[slot, r] for r in range(g_pages)],
                               qcol_ref, ltri_ref, m_ref, l_ref, car_ref, w_ref.at[c % 2],
                               n_heads=n_heads, hd=hd)
        return values(c - 1, token) if c > 0 else token

    x = x_ref[...]
    h = _rms(x, g_ref[...]).astype(BF16)

    def waits_for(token, val):
        if token is None:
            return val
        return val + jnp.tile(token, (val.shape[0] // token.shape[0], val.shape[1] // token.shape[1]))

    def up(c, token=None):
        a = _dot(h, wgu_ref[:, c * chunk:(c + 1) * chunk])
        b = _dot(h, wgu_ref[:, d_ff + c * chunk:d_ff + (c + 1) * chunk])
        act_ref[:, c * chunk:(c + 1) * chunk] = (jax.nn.silu(a) * waits_for(token, b)).astype(BF16)

    def down(nb, token=None):
        cols = slice(nb * down_block, (nb + 1) * down_block)
        o_ref[:, cols] = waits_for(token, x[:, cols]) + 0.5 * _dot(act_ref[...], wd_ref[:, cols])

    def ple(token=None):
        x3 = waits_for(token, o_ref[...])
        gate = jax.nn.sigmoid(_dot(_rms(x3, pg_ref[...]).astype(BF16), wpg_ref[...]))
        o_ref[...] = x3 + gate * _dot(pe_ref[...].astype(BF16), wpp_ref[...])

    units = ([functools.partial(up, c) for c in range(d_ff // chunk)]
             + [functools.partial(down, nb) for nb in range(x.shape[1] // down_block)]
             + ([ple] if with_ple else []))
    for u, unit in enumerate(units):
        token = None
        for c in range(n_chunks):
            if (c * len(units)) // n_chunks == u:
                token = decode(c)
        unit(token) if token is not None else unit()
    for cp in v_copies(step, n_chunks - 1):
        cp.wait()
    values(n_chunks - 1, _order_token(w_ref[(n_chunks - 1) % 2, g_pages]))

    @pl.when(step % steps_per_sample == steps_per_sample - 1)
    def _():
        _decode_finish(q_ref, kn_ref, vn_ref, m_ref, l_ref, acc_ref, a_ref, n_heads=n_heads, hd=hd)


def _ffn_decode_call(x, g, wgu, wd, tm, page_table, b0, n_samples, qcol, q, kn, vn, fn,
                     cache_kt, cache_vt, cache_lf_t, g_pages, n_heads, hd, ple=None):
    n, d = x.shape
    d_ff = wd.shape[0]
    chunk = _ffn_chunk(d_ff)
    n_steps = n // tm
    n_pages = page_table.shape[1]
    n_pool, width, page = cache_kt.shape
    assert n_steps % n_samples == 0
    steps_per_sample = n_steps // n_samples
    assert n_pages % (steps_per_sample * g_pages) == 0
    n_chunks = n_pages // (steps_per_sample * g_pages)
    down_block = 2 * MXU_TILE if d % (2 * MXU_TILE) == 0 else d
    n_slots = CACHE_RING_SLOTS
    assert n_chunks % n_slots == 0, "a chunk's ring slot must not depend on the grid step"
    ltri = jnp.asarray(np.concatenate([np.tril(np.ones((page, page), np.float32), -1),
                                       np.ones((page, page), np.float32)], axis=1), BF16)
    sample = lambda i, pt: (b0 + i // steps_per_sample, 0, 0)
    const = lambda shape: pl.BlockSpec(shape, lambda i, pt: (0,) * len(shape), pipeline_mode=pl.Buffered(1))
    row_spec = pl.BlockSpec((1, 1, width), sample)
    hbm = pl.BlockSpec(memory_space=pl.ANY)
    ple_specs, ple_args = [], ()
    if ple is not None:
        ple_args = tuple(ple)
        ple_specs = ([pl.BlockSpec((tm, ple[0].shape[1]), lambda i, pt: (i, 0))]
                     + [const(t.shape) for t in ple[1:]])
    return pl.pallas_call(
        functools.partial(_ffn_decode_kernel, d_ff=d_ff, chunk=chunk, down_block=down_block, b0=b0, g_pages=g_pages,
                          n_chunks=n_chunks, steps_per_sample=steps_per_sample, n_heads=n_heads, hd=hd,
                          with_ple=ple is not None),
        out_shape=(jax.ShapeDtypeStruct((n, d), F32), jax.ShapeDtypeStruct((n_samples, 1, width), BF16)),
        grid_spec=pltpu.PrefetchScalarGridSpec(
            num_scalar_prefetch=1,
            grid=(n_steps,),
            in_specs=[pl.BlockSpec((tm, d), lambda i, pt: (i, 0)),
                      const(g.shape), const(wgu.shape), const(wd.shape),
                      pl.BlockSpec((1, width, page), sample), row_spec, row_spec, row_spec,
                      pl.BlockSpec((1, n_heads, page), sample), const(ltri.shape), hbm, hbm, hbm] + ple_specs,
            out_specs=(pl.BlockSpec((tm, d), lambda i, pt: (i, 0)),
                       pl.BlockSpec((1, 1, width), lambda i, pt: (i // steps_per_sample, 0, 0))),
            scratch_shapes=[pltpu.VMEM((tm, d_ff), BF16),
                            pltpu.VMEM((n_slots, g_pages, width, page), F32),
                            pltpu.VMEM((n_slots, g_pages, width, page), F32),
                            pltpu.VMEM((n_slots, g_pages, n_heads, page), F32),
                            pltpu.SemaphoreType.DMA((n_slots, 3)),
                            pltpu.VMEM((n_heads, page), F32), pltpu.VMEM((n_heads, page), F32),
                            pltpu.VMEM((width, page), F32), pltpu.VMEM((n_heads, page), F32),
                            pltpu.VMEM((2, g_pages + 1, n_heads, page), F32)]),
        compiler_params=_params(1),
        name="ffn_decode",
    )(page_table, x, g, wgu, wd, qcol, q, kn, vn, fn, ltri, cache_kt, cache_vt, cache_lf_t, *ple_args)


def _merge_kernel(x_ref, a_ref, mg_ref, g_ref, wga_ref, wgb_ref, wpa_ref, wpg_ref, wo_ref, o_ref):
    x = x_ref[...]
    h = _rms(x, g_ref[...]).astype(BF16)
    ga = jax.nn.sigmoid(_dot(h, wga_ref[...]))
    gb = jax.nn.sigmoid(_dot(h, wgb_ref[...]))
    merged = ga * _dot(a_ref[...], wpa_ref[...]) + gb * _dot(mg_ref[...], wpg_ref[...])
    o_ref[...] = x + _dot(merged.astype(BF16), wo_ref[...])


def _merge_call(x, a, mg, g, wga, wgb, wpa, wpg, wo, tm):
    n, d = x.shape
    w = a.shape[1]
    row = lambda cols: pl.BlockSpec((tm, cols), lambda i: (i, 0))
    return pl.pallas_call(
        _merge_kernel,
        out_shape=jax.ShapeDtypeStruct((n, d), F32),
        grid=(n // tm,),
        in_specs=[row(d), row(w), row(mg.shape[1])] + [_const_spec(t.shape) for t in (g, wga, wgb, wpa, wpg, wo)],
        out_specs=row(d),
        compiler_params=_params(1),
        name="merge_out",
    )(x, a, mg, g, wga, wgb, wpa, wpg, wo)


def _ple_kernel(x_ref, p_ref, g_ref, wg_ref, wp_ref, o_ref):
    x = x_ref[...]
    gate = jax.nn.sigmoid(_dot(_rms(x, g_ref[...]).astype(BF16), wg_ref[...]))
    o_ref[...] = x + gate * _dot(p_ref[...].astype(BF16), wp_ref[...])


def _ple_call(x, p, g, wg, wp, tm):
    n, d = x.shape
    row = lambda cols: pl.BlockSpec((tm, cols), lambda i: (i, 0))
    return pl.pallas_call(
        _ple_kernel,
        out_shape=jax.ShapeDtypeStruct((n, d), F32),
        grid=(n // tm,),
        in_specs=[row(d), row(p.shape[1])] + [_const_spec(t.shape) for t in (g, wg, wp)],
        out_specs=row(d),
        compiler_params=_params(1),
        name="ple",
    )(x, p, g, wg, wp)


class _Tiles(NamedTuple):
    rows: int
    merge_rows: int
    attn_block: int
    cache_pages: int


def _tiles(n_tokens, seq):
    rows = 512
    merge_rows = 1024 if n_tokens % 1024 == 0 else rows
    attn_block = 512
    assert n_tokens % rows == 0 and seq % rows == 0 and seq % attn_block == 0
    return _Tiles(rows=rows, merge_rows=merge_rows, attn_block=attn_block, cache_pages=8)


def _block_diag_mean(width, group):
    idx = np.arange(width) // group
    return jnp.asarray((idx[:, None] == idx[None, :]).astype(np.float32) / group, BF16)


def _forget_selectors(n_heads, hd):
    width = n_heads * hd
    selq = np.zeros((LANES, width), np.float32)
    selk = np.zeros((LANES, width), np.float32)
    for h in range(n_heads):
        for p in range(3):
            selq[p * PIECE_STRIDE + h, h * hd + p] = 1.0
            selk[p * PIECE_STRIDE + h, h * hd + 3 + p] = -1.0
            selq[n_heads, h * hd + 3 + p] = 1.0
            selk[n_heads, h * hd + p] = 1.0
    return jnp.asarray(selq.T, BF16), jnp.asarray(selk, BF16)


def kernel(x_prompt, x_sample, cache_k, cache_v, cache_logf, page_table, p_prompt, p_sample,
           ffn1_norm, ffn1_w_gu, ffn1_w_down, mix_norm, w_in, b_forget, q_norm, k_norm,
           gmlp_v_norm, w_spatial, b_spatial, w_proj_attn, w_proj_gmlp, w_out,
           ffn2_norm, ffn2_w_gu, ffn2_w_down, ple_norm, ple_w_gate, ple_w_proj):
    depth = ffn1_norm.shape[0]
    bsz, seq, d = x_prompt.shape
    db = x_sample.shape[0]
    assert x_sample.shape[1] == 1, "the sample group decodes one token per step"
    n_heads, hd = cache_k.shape[3], cache_k.shape[4]
    aw = n_heads * hd
    n_groups, chunk = w_spatial.shape[1], w_spatial.shape[2]
    gw = w_proj_gmlp.shape[1]
    gdim = gw // n_groups
    page = cache_k.shape[2]
    n_pool = cache_k.shape[1]
    assert hd == gdim and aw == gw and LANES % hd == 0 and n_heads < PIECE_STRIDE and page == LANES

    tiles = _tiles(bsz * seq, seq)
    tm, tq, g_pages = tiles.rows, tiles.attn_block, tiles.cache_pages
    scale = hd ** -0.5
    tri = jnp.asarray(np.tril(np.ones((tm, tm), np.float32)), BF16)
    bd = _block_diag_mean(math.gcd(aw, MXU_TILE), hd)
    selqt, selk = _forget_selectors(n_heads, hd)

    xp = x_prompt.reshape(bsz * seq, d)
    xs = x_sample.reshape(db, d)
    outs = {k: [] for k in ("kp", "vp", "fp", "ks", "vs", "fs", "gs")}
    for li in range(depth):
        bf = lambda t: t[li].astype(BF16)
        row = lambda t: t[li].reshape(1, -1).astype(F32)
        tile_h = lambda t, mult=1.0: jnp.tile(t[li].astype(F32) * mult, n_heads).reshape(1, -1)
        w1gu, w1d, w2gu, w2d = bf(ffn1_w_gu), bf(ffn1_w_down), bf(ffn2_w_gu), bf(ffn2_w_down)
        wi = w_in[li]
        o = 0
        wqkv = wi[:, o:o + 3 * aw].astype(BF16); o += 3 * aw
        wf = jnp.pad(wi[:, o:o + n_heads], ((0, 0), (0, LANES - n_heads))).astype(BF16); o += n_heads
        wugv = wi[:, o:o + 2 * gw].astype(BF16); o += 2 * gw
        wga = wi[:, o:o + d].astype(BF16); o += d
        wgb = wi[:, o:o + d].astype(BF16)
        bfg = jnp.pad(b_forget[li].astype(F32), (0, LANES - n_heads)).reshape(1, LANES)
        qg, kg, gvg = tile_h(q_norm, scale * LOG2E), tile_h(k_norm), tile_h(gmlp_v_norm)
        ws = bf(w_spatial)
        bs = jnp.repeat(b_spatial[li].astype(F32).T, gdim, axis=1)
        w00 = jnp.repeat(w_spatial[li, :, 0, 0].astype(F32), gdim).reshape(1, gw)
        b0 = jnp.repeat(b_spatial[li, :, 0].astype(F32), gdim).reshape(1, gw)
        wpa, wpg, wo = bf(w_proj_attn), bf(w_proj_gmlp), bf(w_out)
        plg, plp = bf(ple_w_gate), bf(ple_w_proj)
        mixer_w = (row(mix_norm), wqkv, wf, bfg, wugv, qg, kg, gvg, bd)

        x1s = _ffn_call(xs, row(ffn1_norm), w1gu, w1d, db)
        sconsts = mixer_w + (w00, b0)
        full = lambda cols: pl.BlockSpec((db, cols), lambda i: (0, 0))
        ssd = lambda cols, dt: jax.ShapeDtypeStruct((db, cols), dt)
        qs, ksn, vsn, lfs, gvs, mgs = pl.pallas_call(
            _mixer_sample_kernel,
            out_shape=(ssd(aw, F32), ssd(aw, F32), ssd(aw, F32), ssd(LANES, F32), ssd(gw, F32), ssd(gw, BF16)),
            grid=(1,),
            in_specs=[full(d)] + [_const_spec(t.shape) for t in sconsts],
            out_specs=tuple(full(c) for c in (aw, aw, aw, LANES, gw, gw)),
            compiler_params=_params(1),
            name="mixer_sample",
        )(x1s, *sconsts)
        fn = jnp.broadcast_to(lfs[:, :n_heads, None], (db, n_heads, page))
        qcol = jnp.broadcast_to(qs[:, :, None], (db, aw, page))
        cache_lf_t = jnp.swapaxes(cache_logf[li].astype(F32), 1, 2)
        cache_kt = jnp.transpose(cache_k[li], (0, 2, 3, 1)).reshape(n_pool, aw, page)
        cache_vt = jnp.transpose(cache_v[li], (0, 2, 3, 1)).reshape(n_pool, aw, page)
        decode_args = (qcol, qs.reshape(db, 1, aw), ksn.reshape(db, 1, aw), vsn.reshape(db, 1, aw), fn,
                       cache_kt, cache_vt, cache_lf_t, g_pages, n_heads, hd)
        half = db // 2

        n = bsz * seq
        x1, a_s0 = _ffn_decode_call(xp, row(ffn1_norm), w1gu, w1d, tm, page_table, 0, half, *decode_args)
        rowspec = lambda cols: pl.BlockSpec((tm, cols), lambda i: (i, 0))
        consts = mixer_w + (tri, selqt, selk, ws, bs)
        sds = lambda cols, dt: jax.ShapeDtypeStruct((n, cols), dt)
        tps = seq // tm
        tsd = lambda rows_, dt: jax.ShapeDtypeStruct((bsz, rows_, seq), dt)
        tspec = lambda rows_: pl.BlockSpec((1, rows_, tm), lambda i: (i // tps, 0, i % tps))
        qt, qat, k32t, kp, ka, v32t, vt, lfpt, mg = pl.pallas_call(
            functools.partial(_mixer_prompt_kernel, tiles_per_seq=tps, n_heads=n_heads, chunk=chunk),
            out_shape=(tsd(aw, BF16), tsd(aw, BF16), tsd(aw, F32), sds(aw, BF16), sds(aw, BF16),
                       tsd(aw, F32), tsd(aw, BF16), tsd(n_heads, F32), sds(gw, BF16)),
            grid=(n // tm,),
            in_specs=[rowspec(d)] + [_const_spec(t.shape) for t in consts],
            out_specs=(tspec(aw), tspec(aw), tspec(aw), rowspec(aw), rowspec(aw),
                       tspec(aw), tspec(aw), tspec(n_heads), rowspec(gw)),
            scratch_shapes=[pltpu.VMEM((1, LANES), F32)],
            compiler_params=_params(1),
            name="mixer_prompt",
        )(x1, *consts)
        r3 = lambda t: t.reshape(bsz, seq, aw)
        a = _attn_call(qt, qat, r3(kp), r3(ka), vt, tq, hd).reshape(n, aw)
        x2 = _merge_call(x1, a, mg, row(mix_norm), wga, wgb, wpa, wpg, wo, tiles.merge_rows)
        xp, a_s1 = _ffn_decode_call(x2, row(ffn2_norm), w2gu, w2d, tm, page_table, half, db - half,
                                    *decode_args,
                                    ple=(p_prompt[li].reshape(n, -1), row(ple_norm), plg, plp))
        outs["kp"].append(k32t.reshape(bsz, n_heads, hd, seq).transpose(0, 3, 1, 2))
        outs["vp"].append(v32t.reshape(bsz, n_heads, hd, seq).transpose(0, 3, 1, 2))
        outs["fp"].append(lfpt.transpose(0, 2, 1))

        a_s = jnp.concatenate([a_s0, a_s1], axis=0).reshape(db, aw)
        x2s = _merge_call(x1s, a_s, mgs, row(mix_norm), wga, wgb, wpa, wpg, wo, db)
        x3s = _ffn_call(x2s, row(ffn2_norm), w2gu, w2d, db)
        xs = _ple_call(x3s, p_sample[li].reshape(db, -1), row(ple_norm), plg, plp, db)
        outs["ks"].append(ksn.reshape(db, 1, n_heads, hd))
        outs["vs"].append(vsn.reshape(db, 1, n_heads, hd))
        outs["fs"].append(lfs[:, :n_heads].reshape(db, 1, n_heads))
        outs["gs"].append(gvs.reshape(db, 1, n_groups, gdim))

    st = lambda key: jnp.stack(outs[key])
    return (xp.reshape(bsz, seq, d), xs.reshape(db, 1, d), st("kp"), st("vp"), st("fp"),
            st("ks"), st("vs"), st("fs"), st("gs"))
```
